```python
import jax, jax.numpy as jnp
from jax import lax
import numpy as np

D_MODEL = 1024
BATCH = 4
SEQ = 4096
DEPTH = 4
DEC_BATCH = 32
DEC_SEQ = 4
PAST_LEN = 8192
PAGE_SIZE = 128

N_MIXERS = 2
N_LAYERS_A = (DEPTH + 1) // 2
N_LAYERS_B = DEPTH // 2
HEAD_DIM = 64
N_SLOTS = 8
DIL_WINDOWS = (128, 512, 2048)
DIL_RATES = (1, 4, 16)
N_GROUPS_A = len(DIL_WINDOWS)
QKV_WIDTH = 3 * N_GROUPS_A * N_SLOTS * HEAD_DIM
A_WIDTH = N_SLOTS * HEAD_DIM
BAND_BLOCK = 128
CHUNK = 128
D_FFN_B = 6 * D_MODEL
D_V = D_FFN_B // 2
N_GROUPS_B = 8
GROUP_B = D_V // N_GROUPS_B
D_FF = 2816
RMS_EPS = 1e-6
LN_EPS = 1e-5

kernel_name = "dilated_window_gmlp_macaron_hybrid_step"


def rmsnorm(x, g):
    xf = x.astype(jnp.float32)
    y = xf * lax.rsqrt(jnp.mean(xf * xf, axis=-1, keepdims=True) + RMS_EPS)
    return (y * g.astype(jnp.float32)).astype(x.dtype)


def layernorm(x, g, b):
    xf = x.astype(jnp.float32)
    mu = jnp.mean(xf, axis=-1, keepdims=True)
    xc = xf - mu
    y = xc * lax.rsqrt(jnp.mean(xc * xc, axis=-1, keepdims=True) + LN_EPS)
    return (y * g.astype(jnp.float32) + b.astype(jnp.float32)).astype(x.dtype)


def swiglu(x, w_in, w_out):
    gate, up = jnp.split(x @ w_in, 2, axis=-1)
    return (jax.nn.silu(gate) * up) @ w_out


def project_qkv(h, w_qkv):
    b, s, _ = h.shape
    qkv = (h @ w_qkv).reshape(b, s, 3, N_GROUPS_A, N_SLOTS, HEAD_DIM)
    return qkv[:, :, 0], qkv[:, :, 1], qkv[:, :, 2]


def band_attention(q, k, v, dil, n_back):
    b, s, h, dh = q.shape
    s_sub = s // dil
    nb = -(-s_sub // BAND_BLOCK)
    pad = nb * BAND_BLOCK - s_sub

    def to_sub(a):
        a = a.reshape(b, s_sub, dil, h, dh).transpose(0, 2, 1, 3, 4)
        a = jnp.pad(a, ((0, 0), (0, 0), (0, pad), (0, 0), (0, 0)))
        return a.reshape(b, dil, nb, BAND_BLOCK, h, dh)

    def with_prev(a):
        prev = jnp.pad(a, ((0, 0), (0, 0), (1, 0), (0, 0), (0, 0), (0, 0)))[:, :, :nb]
        return jnp.concatenate([prev, a], axis=3)

    qs = to_sub(q)
    kb = with_prev(to_sub(k))
    vb = with_prev(to_sub(v))
    scores = jnp.einsum('brnqhd,brnkhd->brnhqk', qs, kb).astype(jnp.float32) * (HEAD_DIM ** -0.5)
    qi = jnp.arange(BAND_BLOCK)[:, None]
    ki = jnp.arange(2 * BAND_BLOCK)[None, :]
    dist = BAND_BLOCK + qi - ki
    band = (dist >= 0) & (dist <= n_back)
    key_sub = (jnp.arange(nb)[:, None] - 1) * BAND_BLOCK + ki
    mask = band[None] & (key_sub >= 0)[:, None, :]
    scores = jnp.where(mask[:, None], scores, -jnp.inf)
    m = jnp.max(scores, axis=-1, keepdims=True)
    p = jnp.exp(scores - m)
    denom = jnp.sum(p, axis=-1, keepdims=True)
    o = jnp.einsum('brnhqk,brnkhd->brnqhd', (p / denom).astype(v.dtype), vb).astype(jnp.float32)
    lse = (m + jnp.log(denom))[..., 0]
    o = o.reshape(b, dil, nb * BAND_BLOCK, h, dh)[:, :, :s_sub]
    o = o.transpose(0, 2, 1, 3, 4).reshape(b, s, h, dh)
    lse = lse.transpose(0, 1, 2, 4, 3).reshape(b, dil, nb * BAND_BLOCK, h)[:, :, :s_sub]
    lse = lse.transpose(0, 2, 1, 3).reshape(b, s, h)
    return o, lse


def cached_attention(q, k_new, v_new, k_buf, v_buf, dil, n_back):
    t = q.shape[1]
    buf_len = k_buf.shape[1]
    k_all = jnp.concatenate([k_buf, k_new], axis=1)
    v_all = jnp.concatenate([v_buf, v_new], axis=1)
    idx = buf_len + jnp.arange(t)[:, None] - dil * jnp.arange(n_back + 1)[None, :]
    valid = idx >= 0
    idx = jnp.maximum(idx, 0)
    kg = k_all[:, idx]
    vg = v_all[:, idx]
    scores = jnp.einsum('bthd,btjhd->bthj', q, kg).astype(jnp.float32) * (HEAD_DIM ** -0.5)
    scores = jnp.where(valid[None, :, None, :], scores, -jnp.inf)
    m = jnp.max(scores, axis=-1, keepdims=True)
    p = jnp.exp(scores - m)
    denom = jnp.sum(p, axis=-1, keepdims=True)
    o = jnp.einsum('bthj,btjhd->bthd', (p / denom).astype(v_new.dtype), vg).astype(jnp.float32)
    lse = (m + jnp.log(denom))[..., 0]
    return o, lse


def merge_groups(outs, lses, w_out, dtype):
    o = jnp.stack(outs, axis=2)
    wts = jax.nn.softmax(jnp.stack(lses, axis=2), axis=2)
    y = jnp.einsum('bsgh,bsghd->bshd', wts, o)
    b, s = y.shape[:2]
    return y.reshape(b, s, A_WIDTH).astype(dtype) @ w_out


def dilated_mixer_prompt(h, w_qkv, w_out):
    q, k, v = project_qkv(h, w_qkv)
    s = h.shape[1]
    outs, lses, rows = [], [], []
    for g in range(N_GROUPS_A):
        win, dil = DIL_WINDOWS[g], DIL_RATES[g]
        o, lse = band_attention(q[:, :, g], k[:, :, g], v[:, :, g], dil, win // dil)
        outs.append(o)
        lses.append(lse)
        keep = min(win, s)
        rows.append(jnp.stack([k[:, s - keep:, g], v[:, s - keep:, g]], axis=2))
    return merge_groups(outs, lses, w_out, h.dtype), rows


def dilated_mixer_sample(h, bufs, w_qkv, w_out):
    q, k, v = project_qkv(h, w_qkv)
    outs, lses, rows = [], [], []
    for g in range(N_GROUPS_A):
        win, dil = DIL_WINDOWS[g], DIL_RATES[g]
        buf = bufs[g]
        o, lse = cached_attention(q[:, :, g], k[:, :, g], v[:, :, g],
                                  buf[:, :, 0], buf[:, :, 1], dil, win // dil)
        outs.append(o)
        lses.append(lse)
        rows.append(jnp.stack([k[:, :, g], v[:, :, g]], axis=2))
    return merge_groups(outs, lses, w_out, h.dtype), rows


def chunk_gmlp(h, w_uv, ln_g, ln_b, w_s, b_s, w_out):
    b, s, _ = h.shape
    u, v = jnp.split(jax.nn.gelu(h @ w_uv), 2, axis=-1)
    v = layernorm(v, ln_g, ln_b)
    clen = min(s, CHUNK)
    n = s // clen
    causal = jnp.tril(jnp.ones((clen, clen), dtype=bool))
    w = jnp.where(causal[None], w_s[:, :clen, :clen], 0)
    vr = v.reshape(b, n, clen, N_GROUPS_B, GROUP_B)
    mixed = jnp.einsum('gts,bnsgc->bntgc', w, vr) + b_s[:, :clen].T[None, None, :, :, None]
    return (u * mixed.reshape(b, s, D_V)) @ w_out, v


def setup_inputs(seed: int = 0) -> dict:
    key = jax.random.key(seed)
    ks = jax.random.split(key, 24)
    f32 = jnp.float32

    def nrm(k, shape, scale=1.0):
        return jax.random.normal(k, shape, f32) * scale

    def gain(k, shape):
        return 1.0 + 0.02 * jax.random.normal(k, shape, f32)

    return {
        "x_prompt": nrm(ks[0], (BATCH, SEQ, D_MODEL)),
        "x_sample": nrm(ks[1], (DEC_BATCH, DEC_SEQ, D_MODEL)),
        "cache_kv_w128": nrm(ks[2], (N_LAYERS_A, DEC_BATCH, min(DIL_WINDOWS[0], PAST_LEN), 2, N_SLOTS, HEAD_DIM)),
        "cache_kv_w512": nrm(ks[3], (N_LAYERS_A, DEC_BATCH, min(DIL_WINDOWS[1], PAST_LEN), 2, N_SLOTS, HEAD_DIM)),
        "cache_kv_w2048": nrm(ks[4], (N_LAYERS_A, DEC_BATCH, min(DIL_WINDOWS[2], PAST_LEN), 2, N_SLOTS, HEAD_DIM)),
        "norm_ffn1": gain(ks[5], (DEPTH, D_MODEL)),
        "w_ffn1_in": nrm(ks[6], (DEPTH, D_MODEL, 2 * D_FF), D_MODEL ** -0.5),
        "w_ffn1_out": nrm(ks[7], (DEPTH, D_FF, D_MODEL), D_FF ** -0.5),
        "norm_mix": gain(ks[8], (DEPTH, D_MODEL)),
        "norm_ffn2": gain(ks[9], (DEPTH, D_MODEL)),
        "w_ffn2_in": nrm(ks[10], (DEPTH, D_MODEL, 2 * D_FF), D_MODEL ** -0.5),
        "w_ffn2_out": nrm(ks[11], (DEPTH, D_FF, D_MODEL), D_FF ** -0.5),
        "w_qkv_a": nrm(ks[12], (N_LAYERS_A, D_MODEL, QKV_WIDTH), D_MODEL ** -0.5),
        "w_out_a": nrm(ks[13], (N_LAYERS_A, A_WIDTH, D_MODEL), A_WIDTH ** -0.5),
        "w_uv_b": nrm(ks[14], (N_LAYERS_B, D_MODEL, 2 * D_V), D_MODEL ** -0.5),
        "ln_v_gain": gain(ks[15], (N_LAYERS_B, D_V)),
        "ln_v_bias": nrm(ks[16], (N_LAYERS_B, D_V), 0.02),
        "w_spatial": nrm(ks[17], (N_LAYERS_B, N_GROUPS_B, CHUNK, CHUNK), CHUNK ** -0.5),
        "b_spatial": gain(ks[18], (N_LAYERS_B, N_GROUPS_B, CHUNK)),
        "w_out_b": nrm(ks[19], (N_LAYERS_B, D_V, D_MODEL), D_V ** -0.5),
        "norm_final": gain(ks[20], (D_MODEL,)),
    }


def reference(x_prompt, x_sample, cache_kv_w128, cache_kv_w512, cache_kv_w2048,
              norm_ffn1, w_ffn1_in, w_ffn1_out, norm_mix, norm_ffn2, w_ffn2_in, w_ffn2_out,
              w_qkv_a, w_out_a, w_uv_b, ln_v_gain, ln_v_bias, w_spatial, b_spatial, w_out_b,
              norm_final):
    caches = (cache_kv_w128, cache_kv_w512, cache_kv_w2048)
    xp, xs = x_prompt, x_sample
    kv_prompt = [[] for _ in range(N_GROUPS_A)]
    kv_sample = [[] for _ in range(N_GROUPS_A)]
    v_rows = []
    for i in range(DEPTH):
        li = i // N_MIXERS
        xp = xp + 0.5 * swiglu(rmsnorm(xp, norm_ffn1[i]), w_ffn1_in[i], w_ffn1_out[i])
        xs = xs + 0.5 * swiglu(rmsnorm(xs, norm_ffn1[i]), w_ffn1_in[i], w_ffn1_out[i])
        hp = rmsnorm(xp, norm_mix[i])
        hs = rmsnorm(xs, norm_mix[i])
        if i % N_MIXERS == 0:
            mp, rows_p = dilated_mixer_prompt(hp, w_qkv_a[li], w_out_a[li])
            ms, rows_s = dilated_mixer_sample(hs, [c[li] for c in caches], w_qkv_a[li], w_out_a[li])
            for g in range(N_GROUPS_A):
                kv_prompt[g].append(rows_p[g])
                kv_sample[g].append(rows_s[g])
        else:
            mp, _ = chunk_gmlp(hp, w_uv_b[li], ln_v_gain[li], ln_v_bias[li],
                               w_spatial[li], b_spatial[li], w_out_b[li])
            ms, vs = chunk_gmlp(hs, w_uv_b[li], ln_v_gain[li], ln_v_bias[li],
                                w_spatial[li], b_spatial[li], w_out_b[li])
            v_rows.append(vs)
        xp = xp + mp
        xs = xs + ms
        xp = xp + 0.5 * swiglu(rmsnorm(xp, norm_ffn2[i]), w_ffn2_in[i], w_ffn2_out[i])
        xs = xs + 0.5 * swiglu(rmsnorm(xs, norm_ffn2[i]), w_ffn2_in[i], w_ffn2_out[i])
    y_prompt = rmsnorm(xp, norm_final)
    y_sample = rmsnorm(xs, norm_final)
    kv128_prompt = jnp.stack(kv_prompt[0])
    kv512_prompt = jnp.stack(kv_prompt[1])
    kv2048_prompt = jnp.stack(kv_prompt[2])
    kv128_sample = jnp.stack(kv_sample[0])
    kv512_sample = jnp.stack(kv_sample[1])
    kv2048_sample = jnp.stack(kv_sample[2])
    v_chunk_sample = jnp.stack(v_rows)
    return (y_prompt, y_sample, kv128_prompt, kv512_prompt, kv2048_prompt,
            kv128_sample, kv512_sample, kv2048_sample, v_chunk_sample)
```

```python
import functools

import jax
import jax.numpy as jnp
from jax import lax
from jax.experimental import pallas as pl
from jax.experimental.pallas import tpu as pltpu

F32 = jnp.float32
BF16 = jnp.bfloat16

D_MODEL = 1024
BATCH = 4
SEQ = 4096
DEPTH = 4
DEC_BATCH = 32
DEC_SEQ = 4
HEAD_DIM = 64
N_SLOTS = 8
DIL_WINDOWS = (128, 512, 2048)
DIL_RATES = (1, 4, 16)
N_GROUPS_A = 3
A_WIDTH = N_SLOTS * HEAD_DIM
QKV_WIDTH = 3 * N_GROUPS_A * A_WIDTH
KV_WIDTH = 2 * N_GROUPS_A * A_WIDTH
BAND = 128
CHUNK = 128
D_V = 3072
N_GROUPS_B = 8
GROUP_B = D_V // N_GROUPS_B
D_FF = 2816
RMS_EPS = 1e-6
LN_EPS = 1e-5
NEG = -1e30

VMEM_LIMIT_BYTES = 56 * 1024 * 1024
FF_COLS = 256
UV_COLS = 768
Q_ROWS = 256
NEW_PAD = 16


def _params(n_axes):
    return pltpu.CompilerParams(dimension_semantics=("arbitrary",) * n_axes,
                                vmem_limit_bytes=VMEM_LIMIT_BYTES)


def _resident(shape, index_map):
    return pl.BlockSpec(shape, index_map, pipeline_mode=pl.Buffered(1))


def _rms(x, g):
    return x * lax.rsqrt(jnp.mean(x * x, axis=-1, keepdims=True) + RMS_EPS) * g


def _dot(a, b):
    return jnp.dot(a, b, preferred_element_type=F32)


def _dot_nt(a, b):
    return lax.dot_general(a, b, (((1,), (1,)), ((), ())), preferred_element_type=F32)


def _ffn_kernel(x_ref, g_ref, win_ref, wout_ref, gf_ref, o_ref, a_ref, *, final):
    x = x_ref[...]
    h = _rms(x, g_ref[...]).astype(BF16)
    for c in range(D_FF // FF_COLS):
        lo = c * FF_COLS
        gate = _dot(h, win_ref[:, lo:lo + FF_COLS])
        up = _dot(h, win_ref[:, D_FF + lo:D_FF + lo + FF_COLS])
        a_ref[:, lo:lo + FF_COLS] = (gate * jax.nn.sigmoid(gate) * up).astype(BF16)
    y = x + 0.5 * _dot(a_ref[...], wout_ref[...])
    if final:
        y = _rms(y, gf_ref[...])
    o_ref[...] = y


def _ffn(x, layer, g, w_in, w_out, g_final, *, tm, final=False):
    m = x.shape[0]
    return pl.pallas_call(
        functools.partial(_ffn_kernel, final=final),
        grid=(m // tm,),
        in_specs=[
            pl.BlockSpec((tm, D_MODEL), lambda i: (i, 0)),
            pl.BlockSpec((None, 1, D_MODEL), lambda i: (layer, 0, 0)),
            _resident((None, D_MODEL, 2 * D_FF), lambda i: (layer, 0, 0)),
            _resident((None, D_FF, D_MODEL), lambda i: (layer, 0, 0)),
            pl.BlockSpec((1, D_MODEL), lambda i: (0, 0)),
        ],
        out_specs=pl.BlockSpec((tm, D_MODEL), lambda i: (i, 0)),
        out_shape=jax.ShapeDtypeStruct((m, D_MODEL), F32),
        scratch_shapes=[pltpu.VMEM((tm, D_FF), BF16)],
        compiler_params=_params(1),
        name="ffn",
    )(x, g, w_in, w_out, g_final)


def _qkv_kernel(x_ref, g_ref, w_ref, qkv_ref, kv_ref):
    h = _rms(x_ref[...], g_ref[...]).astype(BF16)
    for c in range(QKV_WIDTH // A_WIDTH):
        lo = c * A_WIDTH
        y = _dot(h, w_ref[:, lo:lo + A_WIDTH])
        if c < N_GROUPS_A:
            qkv_ref[:, lo:lo + A_WIDTH] = (y * (HEAD_DIM ** -0.5)).astype(BF16)
        else:
            qkv_ref[:, lo:lo + A_WIDTH] = y.astype(BF16)
            is_v, grp = divmod(c - N_GROUPS_A, N_GROUPS_A)
            dst = (2 * grp + is_v) * A_WIDTH
            kv_ref[:, dst:dst + A_WIDTH] = y


def _qkv(x, layer, g, w, *, tm):
    m = x.shape[0]
    return pl.pallas_call(
        _qkv_kernel,
        grid=(m // tm,),
        in_specs=[
            pl.BlockSpec((tm, D_MODEL), lambda i: (i, 0)),
            pl.BlockSpec((None, 1, D_MODEL), lambda i: (layer, 0, 0)),
            _resident((None, D_MODEL, QKV_WIDTH), lambda i: (layer // 2, 0, 0)),
        ],
        out_specs=[
            pl.BlockSpec((tm, QKV_WIDTH), lambda i: (i, 0)),
            pl.BlockSpec((tm, KV_WIDTH), lambda i: (i, 0)),
        ],
        out_shape=[jax.ShapeDtypeStruct((m, QKV_WIDTH), BF16),
                   jax.ShapeDtypeStruct((m, KV_WIDTH), F32)],
        compiler_params=_params(1),
        name="qkv",
    )(x, g, w)


def _band_kernel(q_ref, kp_ref, ko_ref, vp_ref, vo_ref, o_ref, l_ref):
    first_key = jnp.where(pl.program_id(2) == 0, BAND, 0)
    qi = lax.broadcasted_iota(jnp.int32, (BAND, 2 * BAND), 0)
    ki = lax.broadcasted_iota(jnp.int32, (BAND, 2 * BAND), 1)
    band = (ki >= qi) & (ki <= qi + BAND)
    band_first = band & (ki >= first_key)
    for h in range(N_SLOTS):
        sl = slice(h * HEAD_DIM, (h + 1) * HEAD_DIM)
        ko = ko_ref[:, sl]
        vo = vo_ref[:, sl]
        for j in range(Q_ROWS // BAND):
            rows = slice(j * BAND, (j + 1) * BAND)
            q = q_ref[rows, sl]
            if j == 0:
                kk = jnp.concatenate([kp_ref[:, sl], ko[:BAND]], axis=0)
                vv = jnp.concatenate([vp_ref[:, sl], vo[:BAND]], axis=0)
                mask = band_first
            else:
                kk, vv, mask = ko, vo, band
            s = jnp.where(mask, _dot_nt(q, kk), NEG)
            mx = jnp.max(s, axis=-1, keepdims=True)
            p = jnp.exp(s - mx)
            den = jnp.sum(p, axis=-1, keepdims=True)
            o_ref[rows, sl] = _dot(p.astype(BF16), vv) / den
            l_ref[rows, sl] = jnp.broadcast_to(mx + jnp.log(den), (BAND, HEAD_DIM))


def _band_attention(qkv, grp):
    d = DIL_RATES[grp]
    rows = BATCH * SEQ // d
    nb = SEQ // d // Q_ROWS
    cols = QKV_WIDTH // A_WIDTH
    view = qkv.reshape(rows, d * QKV_WIDTH)

    def own(part):
        return pl.BlockSpec((Q_ROWS, A_WIDTH),
                            lambda b, r, i: (b * nb + i, r * cols + part * N_GROUPS_A + grp))

    def prev(part):
        return pl.BlockSpec((BAND, A_WIDTH),
                            lambda b, r, i: (jnp.maximum((b * nb + i) * (Q_ROWS // BAND) - 1, 0),
                                             r * cols + part * N_GROUPS_A + grp))

    out_spec = pl.BlockSpec((Q_ROWS, A_WIDTH), lambda b, r, i: (b * nb + i, r))
    out_shape = jax.ShapeDtypeStruct((rows, d * A_WIDTH), F32)
    o, lse = pl.pallas_call(
        _band_kernel,
        grid=(BATCH, d, nb),
        in_specs=[own(0), prev(1), own(1), prev(2), own(2)],
        out_specs=[out_spec, out_spec],
        out_shape=[out_shape, out_shape],
        compiler_params=_params(3),
        name=f"band_attention_d{d}",
    )(view, view, view, view, view)
    return o.reshape(BATCH * SEQ, A_WIDTH), lse.reshape(BATCH * SEQ, A_WIDTH)


def _merge(outs, lses):
    mx = jnp.maximum(jnp.maximum(lses[0], lses[1]), lses[2])
    e = [jnp.exp(l - mx) for l in lses]
    den = e[0] + e[1] + e[2]
    return (e[0] / den) * outs[0] + (e[1] / den) * outs[1] + (e[2] / den) * outs[2]


def _merge_proj_kernel(x_ref, o0, o1, o2, l0, l1, l2, w_ref, out_ref):
    y = _merge([o0[...], o1[...], o2[...]], [l0[...], l1[...], l2[...]])
    out_ref[...] = x_ref[...] + _dot(y.astype(BF16), w_ref[...])


def _merge_proj(x, outs, lses, layer, w, *, tm):
    m = x.shape[0]
    part = pl.BlockSpec((tm, A_WIDTH), lambda i: (i, 0))
    return pl.pallas_call(
        _merge_proj_kernel,
        grid=(m // tm,),
        in_specs=[pl.BlockSpec((tm, D_MODEL), lambda i: (i, 0))] + [part] * 6
                 + [_resident((None, A_WIDTH, D_MODEL), lambda i: (layer // 2, 0, 0))],
        out_specs=pl.BlockSpec((tm, D_MODEL), lambda i: (i, 0)),
        out_shape=jax.ShapeDtypeStruct((m, D_MODEL), F32),
        compiler_params=_params(1),
        name="merge_proj",
    )(x, *outs, *lses, w)


def _sample_attn_kernel(qkv_ref, c0_ref, c1_ref, c2_ref, y_ref):
    n_rows = DEC_SEQ * N_SLOTS
    qkv = qkv_ref[...].astype(F32)
    row_t = lax.broadcasted_iota(jnp.int32, (n_rows, A_WIDTH), 0) // N_SLOTS
    row_h = lax.broadcasted_iota(jnp.int32, (n_rows, A_WIDTH), 0) % N_SLOTS
    col_h = lax.broadcasted_iota(jnp.int32, (n_rows, A_WIDTH), 1) // HEAD_DIM
    own_head = row_h == col_h
    pad = jnp.zeros((NEW_PAD - DEC_SEQ, A_WIDTH), F32)

    def key_mask(n_keys):
        t = lax.broadcasted_iota(jnp.int32, (n_rows, n_keys), 0) // N_SLOTS
        k = lax.broadcasted_iota(jnp.int32, (n_rows, n_keys), 1)
        return t, k

    outs, lses = [], []
    for grp in range(N_GROUPS_A):
        q = qkv[:, grp * A_WIDTH:(grp + 1) * A_WIDTH]
        k_new = qkv[:, (N_GROUPS_A + grp) * A_WIDTH:(N_GROUPS_A + grp + 1) * A_WIDTH]
        v_new = qkv[:, (2 * N_GROUPS_A + grp) * A_WIDTH:(2 * N_GROUPS_A + grp + 1) * A_WIDTH]
        q_rep = jnp.concatenate(
            [jnp.broadcast_to(q[t:t + 1], (N_SLOTS, A_WIDTH)) for t in range(DEC_SEQ)], axis=0)
        q_bd = jnp.where(own_head, q_rep, 0.0).astype(BF16)
        k_new = jnp.concatenate([k_new, pad], axis=0).astype(BF16)
        v_new = jnp.concatenate([v_new, pad], axis=0).astype(BF16)
        if grp == 0:
            kc = c0_ref[:, :A_WIDTH].astype(BF16)
            vc = c0_ref[:, A_WIDTH:].astype(BF16)
            t, k = key_mask(DIL_WINDOWS[0])
            cache_ok = k >= t
            t, k = key_mask(NEW_PAD)
            new_ok = k <= t
        elif grp == 1:
            kc = c1_ref[:, :A_WIDTH].astype(BF16)
            vc = c1_ref[:, A_WIDTH:].astype(BF16)
            t, k = key_mask(DIL_WINDOWS[1])
            cache_ok = (k % DIL_RATES[1]) == t
            t, k = key_mask(NEW_PAD)
            new_ok = k == t
        else:
            w = 2 * A_WIDTH
            kc = jnp.concatenate([c2_ref[:, t * w:t * w + A_WIDTH] for t in range(DEC_SEQ)],
                                 axis=0).astype(BF16)
            vc = jnp.concatenate([c2_ref[:, t * w + A_WIDTH:(t + 1) * w] for t in range(DEC_SEQ)],
                                 axis=0).astype(BF16)
            n_sub = DIL_WINDOWS[2] // DIL_RATES[2]
            t, k = key_mask(DEC_SEQ * n_sub)
            cache_ok = (k // n_sub) == t
            t, k = key_mask(NEW_PAD)
            new_ok = k == t
        s_c = jnp.where(cache_ok, _dot_nt(q_bd, kc), NEG)
        s_n = jnp.where(new_ok, _dot_nt(q_bd, k_new), NEG)
        mx = jnp.maximum(jnp.max(s_c, axis=-1, keepdims=True), jnp.max(s_n, axis=-1, keepdims=True))
        p_c = jnp.exp(s_c - mx)
        p_n = jnp.exp(s_n - mx)
        den = jnp.sum(p_c, axis=-1, keepdims=True) + jnp.sum(p_n, axis=-1, keepdims=True)
        o = (_dot(p_c.astype(BF16), vc) + _dot(p_n.astype(BF16), v_new)) / den
        lse = jnp.broadcast_to(mx + jnp.log(den), (n_rows, A_WIDTH))
        o = jnp.where(own_head, o, 0.0)
        lse = jnp.where(own_head, lse, 0.0)
        outs.append(jnp.concatenate(
            [jnp.sum(o[t * N_SLOTS:(t + 1) * N_SLOTS], axis=0, keepdims=True) for t in range(DEC_SEQ)],
            axis=0))
        lses.append(jnp.concatenate(
            [jnp.sum(lse[t * N_SLOTS:(t + 1) * N_SLOTS], axis=0, keepdims=True) for t in range(DEC_SEQ)],
            axis=0))
    y_ref[...] = _merge(outs, lses)


def _sample_attention(qkv, caches, li):
    w = 2 * A_WIDTH
    c0 = caches[0].reshape(-1, DEC_BATCH, DIL_WINDOWS[0], w)
    c1 = caches[1].reshape(-1, DEC_BATCH, DIL_WINDOWS[1], w)
    n_sub = DIL_WINDOWS[2] // DIL_RATES[2]
    c2 = caches[2].reshape(-1, DEC_BATCH, n_sub, DIL_RATES[2] * w)
    return pl.pallas_call(
        _sample_attn_kernel,
        grid=(DEC_BATCH,),
        in_specs=[
            pl.BlockSpec((None, DEC_SEQ, QKV_WIDTH), lambda b: (b, 0, 0)),
            pl.BlockSpec((None, None, DIL_WINDOWS[0], w), lambda b: (li, b, 0, 0)),
            pl.BlockSpec((None, None, DIL_WINDOWS[1], w), lambda b: (li, b, 0, 0)),
            pl.BlockSpec((None, None, n_sub, DEC_SEQ * w), lambda b: (li, b, 0, 0)),
        ],
        out_specs=pl.BlockSpec((None, DEC_SEQ, A_WIDTH), lambda b: (b, 0, 0)),
        out_shape=jax.ShapeDtypeStruct((DEC_BATCH, DEC_SEQ, A_WIDTH), F32),
        compiler_params=_params(1),
        name="sample_attention",
    )(qkv.reshape(DEC_BATCH, DEC_SEQ, QKV_WIDTH), c0, c1, c2)


def _proj_kernel(x_ref, y_ref, w_ref, out_ref):
    out_ref[...] = x_ref[...] + _dot(y_ref[...].astype(BF16), w_ref[...])


def _proj(x, y, layer, w):
    m = x.shape[0]
    return pl.pallas_call(
        _proj_kernel,
        grid=(1,),
        in_specs=[pl.BlockSpec((m, D_MODEL), lambda i: (0, 0)),
                  pl.BlockSpec((m, A_WIDTH), lambda i: (0, 0)),
                  pl.BlockSpec((None, A_WIDTH, D_MODEL), lambda i: (layer // 2, 0, 0))],
        out_specs=pl.BlockSpec((m, D_MODEL), lambda i: (0, 0)),
        out_shape=jax.ShapeDtypeStruct((m, D_MODEL), F32),
        compiler_params=_params(1),
        name="proj",
    )(x, y, w)


def _gelu(x):
    return jax.nn.gelu(x, approximate=True)


def _gmlp_kernel(x_ref, g_ref, wuv_ref, lng_ref, lnb_ref, ws_ref, bs_ref, wo_ref, *rest,
                 tm, sample):
    if sample:
        out_ref, v_ref, zv_ref, vn_ref, um_ref = rest
    else:
        out_ref, zv_ref, vn_ref, um_ref = rest
    x = x_ref[...]
    h = _rms(x, g_ref[...]).astype(BF16)
    n_uv = D_V // UV_COLS

    tot = jnp.zeros((tm, 1), F32)
    for c in range(n_uv):
        lo = c * UV_COLS
        z = _gelu(_dot(h, wuv_ref[:, D_V + lo:D_V + lo + UV_COLS]))
        zv_ref[:, lo:lo + UV_COLS] = z
        tot = tot + jnp.sum(z, axis=-1, keepdims=True)
    mu = tot / D_V
    sq = jnp.zeros((tm, 1), F32)
    for c in range(n_uv):
        zc = zv_ref[:, c * UV_COLS:(c + 1) * UV_COLS] - mu
        sq = sq + jnp.sum(zc * zc, axis=-1, keepdims=True)
    rstd = lax.rsqrt(sq / D_V + LN_EPS)
    for c in range(n_uv):
        cols = slice(c * UV_COLS, (c + 1) * UV_COLS)
        vn = (zv_ref[:, cols] - mu) * rstd * lng_ref[:, cols] + lnb_ref[:, cols]
        vn_ref[:, cols] = vn.astype(BF16)
        if sample:
            v_ref[:, cols] = vn

    ri = lax.broadcasted_iota(jnp.int32, (CHUNK, CHUNK), 0)
    ci = lax.broadcasted_iota(jnp.int32, (CHUNK, CHUNK), 1)
    causal = ci <= ri
    if sample:
        causal = causal & ((ri // DEC_SEQ) == (ci // DEC_SEQ))
    groups_per_mm = UV_COLS // GROUP_B
    for c in range(n_uv):
        u = _gelu(_dot(h, wuv_ref[:, c * UV_COLS:(c + 1) * UV_COLS]))
        for gl in range(groups_per_mm):
            grp = c * groups_per_mm + gl
            w = jnp.where(causal, ws_ref[grp], 0.0).astype(BF16)
            bias = bs_ref[:, grp:grp + 1]
            cols = slice(grp * GROUP_B, (grp + 1) * GROUP_B)
            for n in range(tm // CHUNK):
                rows = slice(n * CHUNK, (n + 1) * CHUNK)
                mixed = _dot(w, vn_ref[rows, cols]) + bias
                um_ref[rows, cols] = (u[rows, gl * GROUP_B:(gl + 1) * GROUP_B] * mixed).astype(BF16)
    out_ref[...] = x + _dot(um_ref[...], wo_ref[...])


def _gmlp(x, layer, g, w_uv, ln_g, ln_b, w_s, b_s, w_out, *, tm, sample):
    m = x.shape[0]
    li = layer // 2
    out_specs = [pl.BlockSpec((tm, D_MODEL), lambda i: (i, 0))]
    out_shape = [jax.ShapeDtypeStruct((m, D_MODEL), F32)]
    if sample:
        out_specs.append(pl.BlockSpec((tm, D_V), lambda i: (i, 0)))
        out_shape.append(jax.ShapeDtypeStruct((m, D_V), F32))
    return pl.pallas_call(
        functools.partial(_gmlp_kernel, tm=tm, sample=sample),
        grid=(m // tm,),
        in_specs=[
            pl.BlockSpec((tm, D_MODEL), lambda i: (i, 0)),
            pl.BlockSpec((None, 1, D_MODEL), lambda i: (layer, 0, 0)),
            _resident((None, D_MODEL, 2 * D_V), lambda i: (li, 0, 0)),
            pl.BlockSpec((None, 1, D_V), lambda i: (li, 0, 0)),
            pl.BlockSpec((None, 1, D_V), lambda i: (li, 0, 0)),
            pl.BlockSpec((None, N_GROUPS_B, CHUNK, CHUNK), lambda i: (li, 0, 0, 0)),
            pl.BlockSpec((None, CHUNK, N_GROUPS_B), lambda i: (li, 0, 0)),
            _resident((None, D_V, D_MODEL), lambda i: (li, 0, 0)),
        ],
        out_specs=out_specs,
        out_shape=out_shape,
        scratch_shapes=[pltpu.VMEM((tm, D_V), F32), pltpu.VMEM((tm, D_V), BF16),
                        pltpu.VMEM((tm, D_V), BF16)],
        compiler_params=_params(1),
        name="gmlp",
    )(x, g, w_uv, ln_g, ln_b, w_s, b_s, w_out)


def kernel(x_prompt, x_sample, cache_kv_w128, cache_kv_w512, cache_kv_w2048, norm_ffn1, w_ffn1_in,
           w_ffn1_out, norm_mix, norm_ffn2, w_ffn2_in, w_ffn2_out, w_qkv_a, w_out_a, w_uv_b,
           ln_v_gain, ln_v_bias, w_spatial, b_spatial, w_out_b, norm_final):
    caches = (cache_kv_w128, cache_kv_w512, cache_kv_w2048)
    mp, ms = BATCH * SEQ, DEC_BATCH * DEC_SEQ
    tp = 512
    xp = x_prompt.reshape(mp, D_MODEL)
    xs = x_sample.reshape(ms, D_MODEL)

    w1i, w1o, w2i, w2o = (w.astype(BF16) for w in (w_ffn1_in, w_ffn1_out, w_ffn2_in, w_ffn2_out))
    wqkv, woa, wuv, wob = (w.astype(BF16) for w in (w_qkv_a, w_out_a, w_uv_b, w_out_b))
    g1 = norm_ffn1.reshape(DEPTH, 1, D_MODEL)
    gm = norm_mix.reshape(DEPTH, 1, D_MODEL)
    g2 = norm_ffn2.reshape(DEPTH, 1, D_MODEL)
    gf = norm_final.reshape(1, D_MODEL)
    lng = ln_v_gain.reshape(-1, 1, D_V)
    lnb = ln_v_bias.reshape(-1, 1, D_V)
    reps = CHUNK // DEC_SEQ
    ws_p = w_spatial
    bs_p = jnp.swapaxes(b_spatial, 1, 2)
    ws_s = jnp.tile(w_spatial[:, :, :DEC_SEQ, :DEC_SEQ], (1, 1, reps, reps))
    bs_s = jnp.swapaxes(jnp.tile(b_spatial[:, :, :DEC_SEQ], (1, 1, reps)), 1, 2)

    kv_p, kv_s, v_rows = [], [], []
    for i in range(DEPTH):
        li = i // 2
        xp = _ffn(xp, i, g1, w1i, w1o, gf, tm=tp)
        xs = _ffn(xs, i, g1, w1i, w1o, gf, tm=ms)
        if i % 2 == 0:
            qkv, kv = _qkv(xp, i, gm, wqkv, tm=tp)
            kv_p.append(kv)
            parts = [_band_attention(qkv, grp) for grp in range(N_GROUPS_A)]
            xp = _merge_proj(xp, [p[0] for p in parts], [p[1] for p in parts], i, woa, tm=tp)
            qkv, kv = _qkv(xs, i, gm, wqkv, tm=ms)
            kv_s.append(kv)
            y = _sample_attention(qkv, caches, li)
            xs = _proj(xs, y.reshape(ms, A_WIDTH), i, woa)
        else:
            (xp,) = _gmlp(xp, i, gm, wuv, lng, lnb, ws_p, bs_p, wob, tm=256, sample=False)
            xs, v = _gmlp(xs, i, gm, wuv, lng, lnb, ws_s, bs_s, wob, tm=ms, sample=True)
            v_rows.append(v)
        last = i == DEPTH - 1
        xp = _ffn(xp, i, g2, w2i, w2o, gf, tm=tp, final=last)
        xs = _ffn(xs, i, g2, w2i, w2o, gf, tm=ms, final=last)

    def window_rows(kvs, n_batch, n_seq, grp):
        keep = min(DIL_WINDOWS[grp], n_seq)
        return jnp.stack([
            kv.reshape(n_batch, n_seq, N_GROUPS_A, 2, N_SLOTS, HEAD_DIM)[:, n_seq - keep:, grp]
            for kv in kvs])

    return (xp.reshape(BATCH, SEQ, D_MODEL), xs.reshape(DEC_BATCH, DEC_SEQ, D_MODEL),
            window_rows(kv_p, BATCH, SEQ, 0), window_rows(kv_p, BATCH, SEQ, 1),
            window_rows(kv_p, BATCH, SEQ, 2),
            window_rows(kv_s, DEC_BATCH, DEC_SEQ, 0), window_rows(kv_s, DEC_BATCH, DEC_SEQ, 1),
            window_rows(kv_s, DEC_BATCH, DEC_SEQ, 2),
            jnp.stack(v_rows).reshape(len(v_rows), DEC_BATCH, DEC_SEQ, D_V))
```

```python
import functools

import jax
import jax.numpy as jnp
from jax import lax
from jax.experimental import pallas as pl
from jax.experimental.pallas import tpu as pltpu

F32 = jnp.float32
BF16 = jnp.bfloat16

D_MODEL = 1024
BATCH = 4
SEQ = 4096
DEPTH = 4
DEC_BATCH = 32
DEC_SEQ = 4
HEAD_DIM = 64
N_SLOTS = 8
DIL_WINDOWS = (128, 512, 2048)
DIL_RATES = (1, 4, 16)
N_GROUPS_A = 3
A_WIDTH = N_SLOTS * HEAD_DIM
QKV_WIDTH = 3 * N_GROUPS_A * A_WIDTH
KV_WIDTH = 2 * N_GROUPS_A * A_WIDTH
BAND = 128
CHUNK = 128
D_V = 3072
N_GROUPS_B = 8
GROUP_B = D_V // N_GROUPS_B
D_FF = 2816
RMS_EPS = 1e-6
LN_EPS = 1e-5
NEG = -1e30

VMEM_LIMIT_BYTES = 56 * 1024 * 1024
LANES = 128
TILE = 512
TILES_PER_SEQ = SEQ // TILE
FF_COLS = 256
UV_COLS = 768
Q_ROWS = 256
NEW_PAD = 16


def _params(n_axes):
    return pltpu.CompilerParams(dimension_semantics=("arbitrary",) * n_axes,
                                vmem_limit_bytes=VMEM_LIMIT_BYTES)


def _resident(shape, index_map):
    return pl.BlockSpec(shape, index_map, pipeline_mode=pl.Buffered(1))


def _rms(x, g):
    return x * lax.rsqrt(jnp.mean(x * x, axis=-1, keepdims=True) + RMS_EPS) * g


def _dot(a, b):
    return jnp.dot(a, b, preferred_element_type=F32)


def _dot_nt(a, b):
    return lax.dot_general(a, b, (((1,), (1,)), ((), ())), preferred_element_type=F32)


def _ffn_kernel(x_ref, g_ref, win_ref, wout_ref, gf_ref, o_ref, a_ref, *, final):
    x = x_ref[...]
    h = _rms(x, g_ref[...]).astype(BF16)
    for c in range(D_FF // FF_COLS):
        lo = c * FF_COLS
        gate = _dot(h, win_ref[:, lo:lo + FF_COLS])
        up = _dot(h, win_ref[:, D_FF + lo:D_FF + lo + FF_COLS])
        a_ref[:, lo:lo + FF_COLS] = (gate * jax.nn.sigmoid(gate) * up).astype(BF16)
    y = x + 0.5 * _dot(a_ref[...], wout_ref[...])
    if final:
        y = _rms(y, gf_ref[...])
    o_ref[...] = y


def _ffn(x, layer, g, w_in, w_out, g_final, *, tm, final=False):
    m = x.shape[0]
    return pl.pallas_call(
        functools.partial(_ffn_kernel, final=final),
        grid=(m // tm,),
        in_specs=[
            pl.BlockSpec((tm, D_MODEL), lambda i: (i, 0)),
            pl.BlockSpec((None, 1, D_MODEL), lambda i: (layer, 0, 0)),
            _resident((None, D_MODEL, 2 * D_FF), lambda i: (layer, 0, 0)),
            _resident((None, D_FF, D_MODEL), lambda i: (layer, 0, 0)),
            pl.BlockSpec((1, D_MODEL), lambda i: (0, 0)),
        ],
        out_specs=pl.BlockSpec((tm, D_MODEL), lambda i: (i, 0)),
        out_shape=jax.ShapeDtypeStruct((m, D_MODEL), F32),
        scratch_shapes=[pltpu.VMEM((tm, D_FF), BF16)],
        compiler_params=_params(1),
        name="ffn",
    )(x, g, w_in, w_out, g_final)


def _qkv_sample_kernel(x_ref, g_ref, w_ref, qkv_ref, kv_ref):
    h = _rms(x_ref[...], g_ref[...]).astype(BF16)
    for c in range(QKV_WIDTH // A_WIDTH):
        lo = c * A_WIDTH
        part, grp = divmod(c, N_GROUPS_A)
        y = _dot(h, w_ref[:, lo:lo + A_WIDTH])
        if part == 0:
            qkv_ref[:, lo:lo + A_WIDTH] = (y * (HEAD_DIM ** -0.5)).astype(BF16)
        else:
            qkv_ref[:, lo:lo + A_WIDTH] = y.astype(BF16)
            dst = (2 * grp + part - 1) * A_WIDTH
            kv_ref[:, dst:dst + A_WIDTH] = y


def _qkv_sample(x, layer, g, w):
    m = x.shape[0]
    return pl.pallas_call(
        _qkv_sample_kernel,
        grid=(1,),
        in_specs=[
            pl.BlockSpec((m, D_MODEL), lambda i: (0, 0)),
            pl.BlockSpec((None, 1, D_MODEL), lambda i: (layer, 0, 0)),
            pl.BlockSpec((None, D_MODEL, QKV_WIDTH), lambda i: (layer // 2, 0, 0)),
        ],
        out_specs=[
            pl.BlockSpec((m, QKV_WIDTH), lambda i: (0, 0)),
            pl.BlockSpec((m, KV_WIDTH), lambda i: (0, 0)),
        ],
        out_shape=[jax.ShapeDtypeStruct((m, QKV_WIDTH), BF16),
                   jax.ShapeDtypeStruct((m, KV_WIDTH), F32)],
        compiler_params=_params(1),
        name="qkv_sample",
    )(x, g, w)


def _regroup_rows(slab_ref, y, dst_ref, col0, d):
    n = TILE // d
    for cc in range(A_WIDTH // LANES):
        slab_ref[cc] = y[:, cc * LANES:(cc + 1) * LANES]
    for r in range(d):
        for cc in range(A_WIDTH // LANES):
            dst_ref[r * n:(r + 1) * n, col0 + cc * LANES:col0 + (cc + 1) * LANES] = (
                slab_ref[cc, pl.ds(r, n, stride=d), :].astype(BF16))


def _qkv_prompt_kernel(x_ref, g_ref, w_ref, q0_ref, q1_ref, q2_ref, t0_ref, t1_ref, t2_ref,
                       slab_ref):
    tile = pl.program_id(0) % TILES_PER_SEQ
    h = _rms(x_ref[...], g_ref[...]).astype(BF16)
    dst = (q0_ref, q1_ref, q2_ref)
    win = (t0_ref, t1_ref, t2_ref)
    for c in range(QKV_WIDTH // A_WIDTH):
        part, grp = divmod(c, N_GROUPS_A)
        y = _dot(h, w_ref[:, c * A_WIDTH:(c + 1) * A_WIDTH])
        if part == 0:
            y = y * (HEAD_DIM ** -0.5)
        col0 = part * A_WIDTH
        if grp == 0:
            q0_ref[:, col0:col0 + A_WIDTH] = y.astype(BF16)
        else:
            _regroup_rows(slab_ref.at[(grp - 1) * 3 + part], y, dst[grp], col0, DIL_RATES[grp])
        if part > 0:
            keep = min(DIL_WINDOWS[grp], TILE)
            first_tile = TILES_PER_SEQ - max(DIL_WINDOWS[grp] // TILE, 1)

            @pl.when(tile >= first_tile)
            def _(y=y, keep=keep, grp=grp, part=part):
                win[grp][part - 1] = y[TILE - keep:].T.reshape(N_SLOTS, HEAD_DIM, keep)


def _qkv_prompt(x, layer, g, w):
    m = x.shape[0]
    row_spec = pl.BlockSpec((TILE, 3 * A_WIDTH), lambda i: (i, 0))
    win_specs, win_shapes = [], []
    for grp in range(N_GROUPS_A):
        keep = min(DIL_WINDOWS[grp], TILE)
        first_tile = TILES_PER_SEQ - max(DIL_WINDOWS[grp] // TILE, 1)
        win_specs.append(pl.BlockSpec(
            (None, 2, N_SLOTS, HEAD_DIM, keep),
            lambda i, ft=first_tile: (i // TILES_PER_SEQ, 0, 0, 0,
                                      jnp.maximum(i % TILES_PER_SEQ - ft, 0))))
        win_shapes.append(jax.ShapeDtypeStruct((BATCH, 2, N_SLOTS, HEAD_DIM, DIL_WINDOWS[grp]), F32))
    return pl.pallas_call(
        _qkv_prompt_kernel,
        grid=(m // TILE,),
        in_specs=[
            pl.BlockSpec((TILE, D_MODEL), lambda i: (i, 0)),
            pl.BlockSpec((None, 1, D_MODEL), lambda i: (layer, 0, 0)),
            _resident((None, D_MODEL, QKV_WIDTH), lambda i: (layer // 2, 0, 0)),
        ],
        out_specs=[row_spec] * N_GROUPS_A + win_specs,
        out_shape=[jax.ShapeDtypeStruct((m, 3 * A_WIDTH), BF16)] * N_GROUPS_A + win_shapes,
        scratch_shapes=[pltpu.VMEM((6, A_WIDTH // LANES, TILE, LANES), F32)],
        compiler_params=_params(1),
        name="qkv_prompt",
    )(x, g, w)


def _band_kernel(q_ref, kp_ref, ko_ref, vp_ref, vo_ref, o_ref, l_ref):
    n = q_ref.shape[1]
    first_key = jnp.where(pl.program_id(2) == 0, BAND, 0)
    qi = lax.broadcasted_iota(jnp.int32, (BAND, 2 * BAND), 0)
    ki = lax.broadcasted_iota(jnp.int32, (BAND, 2 * BAND), 1)
    band = (ki >= qi) & (ki <= qi + BAND)
    band_first = band & (ki >= first_key)
    q_all = q_ref[...].reshape(Q_ROWS, A_WIDTH)
    ko_all = ko_ref[...].reshape(Q_ROWS, A_WIDTH)
    vo_all = vo_ref[...].reshape(Q_ROWS, A_WIDTH)
    kp_all = kp_ref[...].reshape(BAND, A_WIDTH)
    vp_all = vp_ref[...].reshape(BAND, A_WIDTH)
    for h in range(N_SLOTS):
        sl = slice(h * HEAD_DIM, (h + 1) * HEAD_DIM)
        ko = ko_all[:, sl]
        vo = vo_all[:, sl]
        for j in range(Q_ROWS // BAND):
            q = q_all[j * BAND:(j + 1) * BAND, sl]
            if j == 0:
                kk = jnp.concatenate([kp_all[:, sl], ko[:BAND]], axis=0)
                vv = jnp.concatenate([vp_all[:, sl], vo[:BAND]], axis=0)
                mask = band_first
            else:
                kk, vv, mask = ko, vo, band
            s = jnp.where(mask, _dot_nt(q, kk), NEG)
            mx = jnp.max(s, axis=-1, keepdims=True)
            p = jnp.exp(s - mx)
            den = jnp.sum(p, axis=-1, keepdims=True)
            tiles = slice(j * (BAND // n), (j + 1) * (BAND // n))
            o_ref[tiles, :, sl] = (_dot(p.astype(BF16), vv) / den).reshape(BAND // n, n, HEAD_DIM)
            l_ref[tiles, :, sl] = jnp.broadcast_to(
                mx + jnp.log(den), (BAND, HEAD_DIM)).reshape(BAND // n, n, HEAD_DIM)


def _band_attention(qkv, grp):
    d = DIL_RATES[grp]
    n = min(TILE // d, BAND)
    pieces = BATCH * SEQ // (d * n)
    nb = SEQ // (d * Q_ROWS)
    view = qkv.reshape(pieces, d, n, 3 * A_WIDTH)

    def own(part):
        return pl.BlockSpec((Q_ROWS // n, None, n, A_WIDTH), lambda b, r, i: (b * nb + i, r, 0, part))

    def prev(part):
        return pl.BlockSpec((BAND // n, None, n, A_WIDTH),
                            lambda b, r, i: (jnp.maximum((b * nb + i) * (Q_ROWS // BAND) - 1, 0),
                                             r, 0, part))

    out_spec = pl.BlockSpec((Q_ROWS // n, None, n, A_WIDTH), lambda b, r, i: (b * nb + i, r, 0, 0))
    out_shape = jax.ShapeDtypeStruct((pieces, d, n, A_WIDTH), F32)
    o, lse = pl.pallas_call(
        _band_kernel,
        grid=(BATCH, d, nb),
        in_specs=[own(0), prev(1), own(1), prev(2), own(2)],
        out_specs=[out_spec, out_spec],
        out_shape=[out_shape, out_shape],
        compiler_params=_params(3),
        name=f"band_attention_d{d}",
    )(view, view, view, view, view)
    return o.reshape(BATCH * SEQ, A_WIDTH), lse.reshape(BATCH * SEQ, A_WIDTH)


def _merge(outs, lses):
    mx = jnp.maximum(jnp.maximum(lses[0], lses[1]), lses[2])
    e = [jnp.exp(l - mx) for l in lses]
    den = e[0] + e[1] + e[2]
    return (e[0] / den) * outs[0] + (e[1] / den) * outs[1] + (e[2] / den) * outs[2]


def _position_order(slab_ref, src_ref, d):
    n = TILE // d
    for r in range(d):
        for cc in range(A_WIDTH // LANES):
            slab_ref[cc, pl.ds(r, n, stride=d), :] = src_ref[r * n:(r + 1) * n,
                                                             cc * LANES:(cc + 1) * LANES]
    return [slab_ref[cc] for cc in range(A_WIDTH // LANES)]


def _merge_proj_kernel(x_ref, o0, o1, o2, l0, l1, l2, w_ref, out_ref, slab_ref):
    cols = [slice(cc * LANES, (cc + 1) * LANES) for cc in range(A_WIDTH // LANES)]
    outs = [[o0[:, c] for c in cols],
            _position_order(slab_ref.at[0], o1, DIL_RATES[1]),
            _position_order(slab_ref.at[1], o2, DIL_RATES[2])]
    lses = [[l0[:, c] for c in cols],
            _position_order(slab_ref.at[2], l1, DIL_RATES[1]),
            _position_order(slab_ref.at[3], l2, DIL_RATES[2])]
    y = jnp.concatenate(
        [_merge([o[cc] for o in outs], [l[cc] for l in lses]) for cc in range(len(cols))], axis=1)
    out_ref[...] = x_ref[...] + _dot(y.astype(BF16), w_ref[...])


def _merge_proj(x, outs, lses, layer, w):
    m = x.shape[0]
    part = pl.BlockSpec((TILE, A_WIDTH), lambda i: (i, 0))
    return pl.pallas_call(
        _merge_proj_kernel,
        grid=(m // TILE,),
        in_specs=[pl.BlockSpec((TILE, D_MODEL), lambda i: (i, 0))] + [part] * 6
                 + [_resident((None, A_WIDTH, D_MODEL), lambda i: (layer // 2, 0, 0))],
        out_specs=pl.BlockSpec((TILE, D_MODEL), lambda i: (i, 0)),
        out_shape=jax.ShapeDtypeStruct((m, D_MODEL), F32),
        scratch_shapes=[pltpu.VMEM((4, A_WIDTH // LANES, TILE, LANES), F32)],
        compiler_params=_params(1),
        name="merge_proj",
    )(x, *outs, *lses, w)


def _sample_attn_kernel(qkv_ref, c0_ref, c1_ref, c2_ref, y_ref):
    n_rows = DEC_SEQ * N_SLOTS
    qkv = qkv_ref[...].astype(F32)
    row_h = lax.broadcasted_iota(jnp.int32, (n_rows, A_WIDTH), 0) % N_SLOTS
    col_h = lax.broadcasted_iota(jnp.int32, (n_rows, A_WIDTH), 1) // HEAD_DIM
    own_head = row_h == col_h
    pad = jnp.zeros((NEW_PAD - DEC_SEQ, A_WIDTH), F32)

    def reach(n_keys, offset, d):
        t = lax.broadcasted_iota(jnp.int32, (n_rows, n_keys), 0) // N_SLOTS
        back = t - lax.broadcasted_iota(jnp.int32, (n_rows, n_keys), 1) - offset
        return (back >= 0) & (back <= BAND * d) & ((back & (d - 1)) == 0)

    outs, lses = [], []
    for grp, c_ref in enumerate((c0_ref, c1_ref, c2_ref)):
        d, window = DIL_RATES[grp], DIL_WINDOWS[grp]
        q = qkv[:, grp * A_WIDTH:(grp + 1) * A_WIDTH]
        k_new = qkv[:, (N_GROUPS_A + grp) * A_WIDTH:(N_GROUPS_A + grp + 1) * A_WIDTH]
        v_new = qkv[:, (2 * N_GROUPS_A + grp) * A_WIDTH:(2 * N_GROUPS_A + grp + 1) * A_WIDTH]
        q_rep = jnp.concatenate(
            [jnp.broadcast_to(q[t:t + 1], (N_SLOTS, A_WIDTH)) for t in range(DEC_SEQ)], axis=0)
        q_bd = jnp.where(own_head, q_rep, 0.0).astype(BF16)
        k_new = jnp.concatenate([k_new, pad], axis=0).astype(BF16)
        v_new = jnp.concatenate([v_new, pad], axis=0).astype(BF16)
        s_c = jnp.where(reach(window, -window, d), _dot(q_bd, c_ref[0].astype(BF16)), NEG)
        s_n = jnp.where(reach(NEW_PAD, 0, d), _dot_nt(q_bd, k_new), NEG)
        mx = jnp.maximum(jnp.max(s_c, axis=-1, keepdims=True), jnp.max(s_n, axis=-1, keepdims=True))
        p_c = jnp.exp(s_c - mx)
        p_n = jnp.exp(s_n - mx)
        den = jnp.sum(p_c, axis=-1, keepdims=True) + jnp.sum(p_n, axis=-1, keepdims=True)
        o = (_dot_nt(p_c.astype(BF16), c_ref[1].astype(BF16)) + _dot(p_n.astype(BF16), v_new)) / den
        lse = jnp.broadcast_to(mx + jnp.log(den), (n_rows, A_WIDTH))
        o = jnp.where(own_head, o, 0.0)
        lse = jnp.where(own_head, lse, 0.0)
        outs.append(jnp.concatenate(
            [jnp.sum(o[t * N_SLOTS:(t + 1) * N_SLOTS], axis=0, keepdims=True) for t in range(DEC_SEQ)],
            axis=0))
        lses.append(jnp.concatenate(
            [jnp.sum(lse[t * N_SLOTS:(t + 1) * N_SLOTS], axis=0, keepdims=True) for t in range(DEC_SEQ)],
            axis=0))
    y_ref[...] = _merge(outs, lses)


def _sample_attention(qkv, caches, li):
    views = [jnp.transpose(c, (0, 1, 3, 4, 5, 2)).reshape(-1, DEC_BATCH, 2, A_WIDTH, c.shape[2])
             for c in caches]
    return pl.pallas_call(
        _sample_attn_kernel,
        grid=(DEC_BATCH,),
        in_specs=[pl.BlockSpec((None, DEC_SEQ, QKV_WIDTH), lambda b: (b, 0, 0))] + [
            pl.BlockSpec((None, None, 2, A_WIDTH, w), lambda b: (li, b, 0, 0, 0)) for w in DIL_WINDOWS],
        out_specs=pl.BlockSpec((None, DEC_SEQ, A_WIDTH), lambda b: (b, 0, 0)),
        out_shape=jax.ShapeDtypeStruct((DEC_BATCH, DEC_SEQ, A_WIDTH), F32),
        compiler_params=_params(1),
        name="sample_attention",
    )(qkv.reshape(DEC_BATCH, DEC_SEQ, QKV_WIDTH), *views)


def _proj_kernel(x_ref, y_ref, w_ref, out_ref):
    out_ref[...] = x_ref[...] + _dot(y_ref[...].astype(BF16), w_ref[...])


def _proj(x, y, layer, w):
    m = x.shape[0]
    return pl.pallas_call(
        _proj_kernel,
        grid=(1,),
        in_specs=[pl.BlockSpec((m, D_MODEL), lambda i: (0, 0)),
                  pl.BlockSpec((m, A_WIDTH), lambda i: (0, 0)),
                  pl.BlockSpec((None, A_WIDTH, D_MODEL), lambda i: (layer // 2, 0, 0))],
        out_specs=pl.BlockSpec((m, D_MODEL), lambda i: (0, 0)),
        out_shape=jax.ShapeDtypeStruct((m, D_MODEL), F32),
        compiler_params=_params(1),
        name="proj",
    )(x, y, w)


def _gelu(x):
    return jax.nn.gelu(x, approximate=True)


def _gmlp_kernel(x_ref, g_ref, wuv_ref, lng_ref, lnb_ref, ws_ref, bs_ref, wo_ref, *rest,
                 tm, sample):
    if sample:
        out_ref, v_ref, zv_ref, vn_ref, um_ref = rest
    else:
        out_ref, zv_ref, vn_ref, um_ref = rest
    x = x_ref[...]
    h = _rms(x, g_ref[...]).astype(BF16)
    n_uv = D_V // UV_COLS

    tot = jnp.zeros((tm, 1), F32)
    for c in range(n_uv):
        lo = c * UV_COLS
        z = _gelu(_dot(h, wuv_ref[:, D_V + lo:D_V + lo + UV_COLS]))
        zv_ref[:, lo:lo + UV_COLS] = z
        tot = tot + jnp.sum(z, axis=-1, keepdims=True)
    mu = tot / D_V
    sq = jnp.zeros((tm, 1), F32)
    for c in range(n_uv):
        zc = zv_ref[:, c * UV_COLS:(c + 1) * UV_COLS] - mu
        sq = sq + jnp.sum(zc * zc, axis=-1, keepdims=True)
    rstd = lax.rsqrt(sq / D_V + LN_EPS)
    for c in range(n_uv):
        cols = slice(c * UV_COLS, (c + 1) * UV_COLS)
        vn = (zv_ref[:, cols] - mu) * rstd * lng_ref[:, cols] + lnb_ref[:, cols]
        vn_ref[:, cols] = vn.astype(BF16)
        if sample:
            v_ref[:, cols] = vn

    ri = lax.broadcasted_iota(jnp.int32, (CHUNK, CHUNK), 0)
    ci = lax.broadcasted_iota(jnp.int32, (CHUNK, CHUNK), 1)
    causal = ci <= ri
    if sample:
        causal = causal & ((ri // DEC_SEQ) == (ci // DEC_SEQ))
    groups_per_mm = UV_COLS // GROUP_B
    for c in range(n_uv):
        u = _gelu(_dot(h, wuv_ref[:, c * UV_COLS:(c + 1) * UV_COLS]))
        for gl in range(groups_per_mm):
            grp = c * groups_per_mm + gl
            w = jnp.where(causal, ws_ref[grp], 0.0).astype(BF16)
            bias = bs_ref[:, grp:grp + 1]
            cols = slice(grp * GROUP_B, (grp + 1) * GROUP_B)
            for n in range(tm // CHUNK):
                rows = slice(n * CHUNK, (n + 1) * CHUNK)
                mixed = _dot(w, vn_ref[rows, cols]) + bias
                um_ref[rows, cols] = (u[rows, gl * GROUP_B:(gl + 1) * GROUP_B] * mixed).astype(BF16)
    out_ref[...] = x + _dot(um_ref[...], wo_ref[...])


def _gmlp(x, layer, g, w_uv, ln_g, ln_b, w_s, b_s, w_out, *, tm, sample):
    m = x.shape[0]
    li = layer // 2
    out_specs = [pl.BlockSpec((tm, D_MODEL), lambda i: (i, 0))]
    out_shape = [jax.ShapeDtypeStruct((m, D_MODEL), F32)]
    if sample:
        out_specs.append(pl.BlockSpec((tm, D_V), lambda i: (i, 0)))
        out_shape.append(jax.ShapeDtypeStruct((m, D_V), F32))
    return pl.pallas_call(
        functools.partial(_gmlp_kernel, tm=tm, sample=sample),
        grid=(m // tm,),
        in_specs=[
            pl.BlockSpec((tm, D_MODEL), lambda i: (i, 0)),
            pl.BlockSpec((None, 1, D_MODEL), lambda i: (layer, 0, 0)),
            _resident((None, D_MODEL, 2 * D_V), lambda i: (li, 0, 0)),
            pl.BlockSpec((None, 1, D_V), lambda i: (li, 0, 0)),
            pl.BlockSpec((None, 1, D_V), lambda i: (li, 0, 0)),
            pl.BlockSpec((None, N_GROUPS_B, CHUNK, CHUNK), lambda i: (li, 0, 0, 0)),
            pl.BlockSpec((None, CHUNK, N_GROUPS_B), lambda i: (li, 0, 0)),
            _resident((None, D_V, D_MODEL), lambda i: (li, 0, 0)),
        ],
        out_specs=out_specs,
        out_shape=out_shape,
        scratch_shapes=[pltpu.VMEM((tm, D_V), F32), pltpu.VMEM((tm, D_V), BF16),
                        pltpu.VMEM((tm, D_V), BF16)],
        compiler_params=_params(1),
        name="gmlp",
    )(x, g, w_uv, ln_g, ln_b, w_s, b_s, w_out)


def kernel(x_prompt, x_sample, cache_kv_w128, cache_kv_w512, cache_kv_w2048, norm_ffn1, w_ffn1_in,
           w_ffn1_out, norm_mix, norm_ffn2, w_ffn2_in, w_ffn2_out, w_qkv_a, w_out_a, w_uv_b,
           ln_v_gain, ln_v_bias, w_spatial, b_spatial, w_out_b, norm_final):
    caches = (cache_kv_w128, cache_kv_w512, cache_kv_w2048)
    mp, ms = BATCH * SEQ, DEC_BATCH * DEC_SEQ
    xp = x_prompt.reshape(mp, D_MODEL)
    xs = x_sample.reshape(ms, D_MODEL)

    w1i, w1o, w2i, w2o = (w.astype(BF16) for w in (w_ffn1_in, w_ffn1_out, w_ffn2_in, w_ffn2_out))
    wqkv, woa, wuv, wob = (w.astype(BF16) for w in (w_qkv_a, w_out_a, w_uv_b, w_out_b))
    g1 = norm_ffn1.reshape(DEPTH, 1, D_MODEL)
    gm = norm_mix.reshape(DEPTH, 1, D_MODEL)
    g2 = norm_ffn2.reshape(DEPTH, 1, D_MODEL)
    gf = norm_final.reshape(1, D_MODEL)
    lng = ln_v_gain.reshape(-1, 1, D_V)
    lnb = ln_v_bias.reshape(-1, 1, D_V)
    reps = CHUNK // DEC_SEQ
    ws_p = w_spatial
    bs_p = jnp.swapaxes(b_spatial, 1, 2)
    ws_s = jnp.tile(w_spatial[:, :, :DEC_SEQ, :DEC_SEQ], (1, 1, reps, reps))
    bs_s = jnp.swapaxes(jnp.tile(b_spatial[:, :, :DEC_SEQ], (1, 1, reps)), 1, 2)

    win_p = [[] for _ in range(N_GROUPS_A)]
    kv_s, v_rows = [], []
    for i in range(DEPTH):
        li = i // 2
        xp = _ffn(xp, i, g1, w1i, w1o, gf, tm=TILE)
        xs = _ffn(xs, i, g1, w1i, w1o, gf, tm=ms)
        if i % 2 == 0:
            *qkv_g, t0, t1, t2 = _qkv_prompt(xp, i, gm, wqkv)
            for grp, t in enumerate((t0, t1, t2)):
                win_p[grp].append(t)
            parts = [_band_attention(qkv_g[grp], grp) for grp in range(N_GROUPS_A)]
            xp = _merge_proj(xp, [p[0] for p in parts], [p[1] for p in parts], i, woa)
            qkv, kv = _qkv_sample(xs, i, gm, wqkv)
            kv_s.append(kv)
            y = _sample_attention(qkv, caches, li)
            xs = _proj(xs, y.reshape(ms, A_WIDTH), i, woa)
        else:
            (xp,) = _gmlp(xp, i, gm, wuv, lng, lnb, ws_p, bs_p, wob, tm=256, sample=False)
            xs, v = _gmlp(xs, i, gm, wuv, lng, lnb, ws_s, bs_s, wob, tm=ms, sample=True)
            v_rows.append(v)
        last = i == DEPTH - 1
        xp = _ffn(xp, i, g2, w2i, w2o, gf, tm=TILE, final=last)
        xs = _ffn(xs, i, g2, w2i, w2o, gf, tm=ms, final=last)

    def prompt_window(grp):
        return jnp.transpose(jnp.stack(win_p[grp]), (0, 1, 5, 2, 3, 4))

    def sample_rows(grp):
        return jnp.stack([kv.reshape(DEC_BATCH, DEC_SEQ, N_GROUPS_A, 2, N_SLOTS, HEAD_DIM)[:, :, grp]
                          for kv in kv_s])

    return (xp.reshape(BATCH, SEQ, D_MODEL), xs.reshape(DEC_BATCH, DEC_SEQ, D_MODEL),
            prompt_window(0), prompt_window(1), prompt_window(2),
            sample_rows(0), sample_rows(1), sample_rows(2),
            jnp.stack(v_rows).reshape(len(v_rows), DEC_BATCH, DEC_SEQ, D_V))
```

```python
import functools

import jax
import jax.numpy as jnp
from jax import lax
from jax.experimental import pallas as pl
from jax.experimental.pallas import tpu as pltpu

F32 = jnp.float32
BF16 = jnp.bfloat16

D_MODEL = 1024
BATCH = 4
SEQ = 4096
DEPTH = 4
DEC_BATCH = 32
DEC_SEQ = 4
HEAD_DIM = 64
N_SLOTS = 8
DIL_WINDOWS = (128, 512, 2048)
DIL_RATES = (1, 4, 16)
N_GROUPS_A = 3
A_WIDTH = N_SLOTS * HEAD_DIM
QKV_WIDTH = 3 * N_GROUPS_A * A_WIDTH
KV_WIDTH = 2 * N_GROUPS_A * A_WIDTH
BAND = 128
CHUNK = 128
D_V = 3072
N_GROUPS_B = 8
GROUP_B = D_V // N_GROUPS_B
D_FF = 2816
RMS_EPS = 1e-6
LN_EPS = 1e-5
NEG = -1e30

VMEM_LIMIT_BYTES = 56 * 1024 * 1024
LANES = 128
TILE = 512
TILES_PER_SEQ = SEQ // TILE
FF_COLS = 256
UV_COLS = 768
Q_ROWS = 256
NEW_PAD = 16


def _params(n_axes):
    return pltpu.CompilerParams(dimension_semantics=("arbitrary",) * n_axes,
                                vmem_limit_bytes=VMEM_LIMIT_BYTES)


def _resident(shape, index_map):
    return pl.BlockSpec(shape, index_map, pipeline_mode=pl.Buffered(1))


def _rms(x, g):
    return x * lax.rsqrt(jnp.mean(x * x, axis=-1, keepdims=True) + RMS_EPS) * g


def _dot(a, b):
    return jnp.dot(a, b, preferred_element_type=F32)


def _dot_nt(a, b):
    return lax.dot_general(a, b, (((1,), (1,)), ((), ())), preferred_element_type=F32)


def _ffn_kernel(x_ref, g_ref, win_ref, wout_ref, gf_ref, o_ref, a_ref, *, final):
    x = x_ref[...]
    h = _rms(x, g_ref[...]).astype(BF16)
    for c in range(D_FF // FF_COLS):
        lo = c * FF_COLS
        gate = _dot(h, win_ref[:, lo:lo + FF_COLS])
        up = _dot(h, win_ref[:, D_FF + lo:D_FF + lo + FF_COLS])
        a_ref[:, lo:lo + FF_COLS] = (gate * jax.nn.sigmoid(gate) * up).astype(BF16)
    y = x + 0.5 * _dot(a_ref[...], wout_ref[...])
    if final:
        y = _rms(y, gf_ref[...])
    o_ref[...] = y


def _ffn(x, layer, g, w_in, w_out, g_final, *, tm, final=False):
    m = x.shape[0]
    return pl.pallas_call(
        functools.partial(_ffn_kernel, final=final),
        grid=(m // tm,),
        in_specs=[
            pl.BlockSpec((tm, D_MODEL), lambda i: (i, 0)),
            pl.BlockSpec((None, 1, D_MODEL), lambda i: (layer, 0, 0)),
            _resident((None, D_MODEL, 2 * D_FF), lambda i: (layer, 0, 0)),
            _resident((None, D_FF, D_MODEL), lambda i: (layer, 0, 0)),
            pl.BlockSpec((1, D_MODEL), lambda i: (0, 0)),
        ],
        out_specs=pl.BlockSpec((tm, D_MODEL), lambda i: (i, 0)),
        out_shape=jax.ShapeDtypeStruct((m, D_MODEL), F32),
        scratch_shapes=[pltpu.VMEM((tm, D_FF), BF16)],
        compiler_params=_params(1),
        name="ffn",
    )(x, g, w_in, w_out, g_final)


def _qkv_sample_kernel(x_ref, g_ref, w_ref, qkv_ref, kv_ref):
    h = _rms(x_ref[...], g_ref[...]).astype(BF16)
    for c in range(QKV_WIDTH // A_WIDTH):
        lo = c * A_WIDTH
        part, grp = divmod(c, N_GROUPS_A)
        y = _dot(h, w_ref[:, lo:lo + A_WIDTH])
        if part == 0:
            qkv_ref[:, lo:lo + A_WIDTH] = (y * (HEAD_DIM ** -0.5)).astype(BF16)
        else:
            qkv_ref[:, lo:lo + A_WIDTH] = y.astype(BF16)
            dst = (2 * grp + part - 1) * A_WIDTH
            kv_ref[:, dst:dst + A_WIDTH] = y


def _qkv_sample(x, layer, g, w):
    m = x.shape[0]
    return pl.pallas_call(
        _qkv_sample_kernel,
        grid=(1,),
        in_specs=[
            pl.BlockSpec((m, D_MODEL), lambda i: (0, 0)),
            pl.BlockSpec((None, 1, D_MODEL), lambda i: (layer, 0, 0)),
            pl.BlockSpec((None, D_MODEL, QKV_WIDTH), lambda i: (layer // 2, 0, 0)),
        ],
        out_specs=[
            pl.BlockSpec((m, QKV_WIDTH), lambda i: (0, 0)),
            pl.BlockSpec((m, KV_WIDTH), lambda i: (0, 0)),
        ],
        out_shape=[jax.ShapeDtypeStruct((m, QKV_WIDTH), BF16),
                   jax.ShapeDtypeStruct((m, KV_WIDTH), F32)],
        compiler_params=_params(1),
        name="qkv_sample",
    )(x, g, w)


def _regroup_rows(slab_ref, y, dst_ref, col0, d):
    n = TILE // d
    for cc in range(A_WIDTH // LANES):
        slab_ref[cc] = y[:, cc * LANES:(cc + 1) * LANES]
    for r in range(d):
        for cc in range(A_WIDTH // LANES):
            dst_ref[r * n:(r + 1) * n, col0 + cc * LANES:col0 + (cc + 1) * LANES] = (
                slab_ref[cc, pl.ds(r, n, stride=d), :].astype(BF16))


def _qkv_prompt_kernel(x_ref, g_ref, w_ref, q0_ref, q1_ref, q2_ref, t0_ref, t1_ref, t2_ref,
                       slab_ref):
    h = _rms(x_ref[...], g_ref[...]).astype(BF16)
    dst = (q0_ref, q1_ref, q2_ref)
    win = (t0_ref, t1_ref, t2_ref)
    for c in range(QKV_WIDTH // A_WIDTH):
        part, grp = divmod(c, N_GROUPS_A)
        y = _dot(h, w_ref[:, c * A_WIDTH:(c + 1) * A_WIDTH])
        if part == 0:
            y = y * (HEAD_DIM ** -0.5)
        col0 = part * A_WIDTH
        if grp == 0:
            q0_ref[:, col0:col0 + A_WIDTH] = y.astype(BF16)
        else:
            _regroup_rows(slab_ref.at[(grp - 1) * 3 + part], y, dst[grp], col0, DIL_RATES[grp])
        if part > 0:
            keep = min(DIL_WINDOWS[grp], TILE)
            win[grp][part - 1] = y[TILE - keep:].T.reshape(N_SLOTS, HEAD_DIM, keep)


def _qkv_prompt(x, layer, g, w):
    m = x.shape[0]
    row_spec = pl.BlockSpec((TILE, 3 * A_WIDTH), lambda i: (i, 0))
    win_specs, win_shapes = [], []
    for grp in range(N_GROUPS_A):
        keep = min(DIL_WINDOWS[grp], TILE)
        first_tile = TILES_PER_SEQ - max(DIL_WINDOWS[grp] // TILE, 1)
        win_specs.append(pl.BlockSpec(
            (None, 2, N_SLOTS, HEAD_DIM, keep),
            lambda i, ft=first_tile: (i // TILES_PER_SEQ, 0, 0, 0,
                                      jnp.maximum(i % TILES_PER_SEQ - ft, 0))))
        win_shapes.append(jax.ShapeDtypeStruct((BATCH, 2, N_SLOTS, HEAD_DIM, DIL_WINDOWS[grp]), F32))
    return pl.pallas_call(
        _qkv_prompt_kernel,
        grid=(m // TILE,),
        in_specs=[
            pl.BlockSpec((TILE, D_MODEL), lambda i: (i, 0)),
            pl.BlockSpec((None, 1, D_MODEL), lambda i: (layer, 0, 0)),
            _resident((None, D_MODEL, QKV_WIDTH), lambda i: (layer // 2, 0, 0)),
        ],
        out_specs=[row_spec] * N_GROUPS_A + win_specs,
        out_shape=[jax.ShapeDtypeStruct((m, 3 * A_WIDTH), BF16)] * N_GROUPS_A + win_shapes,
        scratch_shapes=[pltpu.VMEM((6, A_WIDTH // LANES, TILE, LANES), F32)],
        compiler_params=_params(1),
        name="qkv_prompt",
    )(x, g, w)


def _band_kernel(q_ref, kp_ref, ko_ref, vp_ref, vo_ref, o_ref, l_ref, s_ref, p_ref):
    n = q_ref.shape[1]
    n_sub, n_pair = Q_ROWS // BAND, A_WIDTH // LANES
    first_key = jnp.where(pl.program_id(2) == 0, BAND, 0)
    qi = lax.broadcasted_iota(jnp.int32, (2 * BAND, 2 * BAND), 0) % BAND
    ki = lax.broadcasted_iota(jnp.int32, (2 * BAND, 2 * BAND), 1)
    band = (ki >= qi) & (ki <= qi + BAND)
    band_first = band & (ki >= first_key)
    low = lax.broadcasted_iota(jnp.int32, (1, LANES), 1) < HEAD_DIM
    q_all = q_ref[...].reshape(Q_ROWS, A_WIDTH)
    ko_all = ko_ref[...].reshape(Q_ROWS, A_WIDTH)
    vo_all = vo_ref[...].reshape(Q_ROWS, A_WIDTH)
    kp_all = kp_ref[...].reshape(BAND, A_WIDTH)
    vp_all = vp_ref[...].reshape(BAND, A_WIDTH)

    def keys(prev, own, j, cs):
        if j == 0:
            return jnp.concatenate([prev[:, cs], own[:BAND, cs]], axis=0)
        return own[:, cs]

    for pr in range(n_pair):
        cs = slice(pr * LANES, (pr + 1) * LANES)
        for j in range(n_sub):
            q = q_all[j * BAND:(j + 1) * BAND, cs]
            zero = jnp.zeros_like(q)
            q_ab = jnp.concatenate([jnp.where(low, q, zero), jnp.where(low, zero, q)], axis=0)
            s = _dot_nt(q_ab, keys(kp_all, ko_all, j, cs))
            s_ref[pr * n_sub + j] = jnp.where(band_first if j == 0 else band, s, NEG)

    s = s_ref[...]
    mx = jnp.max(s, axis=-1, keepdims=True)
    p_ref[...] = jnp.exp(s - mx).astype(BF16)

    one = jnp.ones((2 * BAND, LANES), BF16)
    for pr in range(n_pair):
        cs = slice(pr * LANES, (pr + 1) * LANES)
        for j in range(n_sub):
            u = pr * n_sub + j
            vv = keys(vp_all, vo_all, j, cs)
            oa = _dot(p_ref[u, :BAND], jnp.where(low, vv, one))
            ob = _dot(p_ref[u, BAND:], jnp.where(low, one, vv))
            den = pltpu.roll(jnp.where(low, ob, oa), HEAD_DIM, axis=1)
            lse = jnp.where(low, mx[u, :BAND], mx[u, BAND:]) + jnp.log(den)
            tiles = slice(j * (BAND // n), (j + 1) * (BAND // n))
            o_ref[tiles, :, cs] = (jnp.where(low, oa, ob) / den).reshape(BAND // n, n, LANES)
            l_ref[tiles, :, cs] = lse.reshape(BAND // n, n, LANES)


def _band_attention(qkv, grp):
    d = DIL_RATES[grp]
    n = min(TILE // d, BAND)
    pieces = BATCH * SEQ // (d * n)
    nb = SEQ // (d * Q_ROWS)
    units = (Q_ROWS // BAND) * (A_WIDTH // LANES)
    view = qkv.reshape(pieces, d, n, 3 * A_WIDTH)

    def own(part):
        return pl.BlockSpec((Q_ROWS // n, None, n, A_WIDTH), lambda b, r, i: (b * nb + i, r, 0, part))

    def prev(part):
        return pl.BlockSpec((BAND // n, None, n, A_WIDTH),
                            lambda b, r, i: (jnp.maximum((b * nb + i) * (Q_ROWS // BAND) - 1, 0),
                                             r, 0, part))

    out_spec = pl.BlockSpec((Q_ROWS // n, None, n, A_WIDTH), lambda b, r, i: (b * nb + i, r, 0, 0))
    out_shape = jax.ShapeDtypeStruct((pieces, d, n, A_WIDTH), F32)
    o, lse = pl.pallas_call(
        _band_kernel,
        grid=(BATCH, d, nb),
        in_specs=[own(0), prev(1), own(1), prev(2), own(2)],
        out_specs=[out_spec, out_spec],
        out_shape=[out_shape, out_shape],
        scratch_shapes=[pltpu.VMEM((units, 2 * BAND, 2 * BAND), F32),
                        pltpu.VMEM((units, 2 * BAND, 2 * BAND), BF16)],
        compiler_params=_params(3),
        name=f"band_attention_d{d}",
    )(view, view, view, view, view)
    return o.reshape(BATCH * SEQ, A_WIDTH), lse.reshape(BATCH * SEQ, A_WIDTH)


def _merge(outs, lses):
    mx = jnp.maximum(jnp.maximum(lses[0], lses[1]), lses[2])
    e = [jnp.exp(l - mx) for l in lses]
    den = e[0] + e[1] + e[2]
    return (e[0] / den) * outs[0] + (e[1] / den) * outs[1] + (e[2] / den) * outs[2]


def _position_order(slab_ref, src_ref, d):
    n = TILE // d
    for r in range(d):
        for cc in range(A_WIDTH // LANES):
            slab_ref[cc, pl.ds(r, n, stride=d), :] = src_ref[r * n:(r + 1) * n,
                                                             cc * LANES:(cc + 1) * LANES]
    return [slab_ref[cc] for cc in range(A_WIDTH // LANES)]


def _merge_proj_kernel(x_ref, o0, o1, o2, l0, l1, l2, w_ref, out_ref, slab_ref):
    cols = [slice(cc * LANES, (cc + 1) * LANES) for cc in range(A_WIDTH // LANES)]
    outs = [[o0[:, c] for c in cols],
            _position_order(slab_ref.at[0], o1, DIL_RATES[1]),
            _position_order(slab_ref.at[1], o2, DIL_RATES[2])]
    lses = [[l0[:, c] for c in cols],
            _position_order(slab_ref.at[2], l1, DIL_RATES[1]),
            _position_order(slab_ref.at[3], l2, DIL_RATES[2])]
    y = jnp.concatenate(
        [_merge([o[cc] for o in outs], [l[cc] for l in lses]) for cc in range(len(cols))], axis=1)
    out_ref[...] = x_ref[...] + _dot(y.astype(BF16), w_ref[...])


def _merge_proj(x, outs, lses, layer, w):
    m = x.shape[0]
    part = pl.BlockSpec((TILE, A_WIDTH), lambda i: (i, 0))
    return pl.pallas_call(
        _merge_proj_kernel,
        grid=(m // TILE,),
        in_specs=[pl.BlockSpec((TILE, D_MODEL), lambda i: (i, 0))] + [part] * 6
                 + [_resident((None, A_WIDTH, D_MODEL), lambda i: (layer // 2, 0, 0))],
        out_specs=pl.BlockSpec((TILE, D_MODEL), lambda i: (i, 0)),
        out_shape=jax.ShapeDtypeStruct((m, D_MODEL), F32),
        scratch_shapes=[pltpu.VMEM((4, A_WIDTH // LANES, TILE, LANES), F32)],
        compiler_params=_params(1),
        name="merge_proj",
    )(x, *outs, *lses, w)


def _sample_attn_kernel(qkv_ref, c0_ref, c1_ref, c2_ref, y_ref):
    n_rows = DEC_SEQ * N_SLOTS
    qkv = qkv_ref[...].astype(F32)
    row_h = lax.broadcasted_iota(jnp.int32, (n_rows, A_WIDTH), 0) % N_SLOTS
    col_h = lax.broadcasted_iota(jnp.int32, (n_rows, A_WIDTH), 1) // HEAD_DIM
    own_head = row_h == col_h
    pad = jnp.zeros((NEW_PAD - DEC_SEQ, A_WIDTH), F32)

    def reach(n_keys, offset, d):
        t = lax.broadcasted_iota(jnp.int32, (n_rows, n_keys), 0) // N_SLOTS
        back = t - lax.broadcasted_iota(jnp.int32, (n_rows, n_keys), 1) - offset
        return (back >= 0) & (back <= BAND * d) & ((back & (d - 1)) == 0)

    outs, lses = [], []
    for grp, c_ref in enumerate((c0_ref, c1_ref, c2_ref)):
        d, window = DIL_RATES[grp], DIL_WINDOWS[grp]
        q = qkv[:, grp * A_WIDTH:(grp + 1) * A_WIDTH]
        k_new = qkv[:, (N_GROUPS_A + grp) * A_WIDTH:(N_GROUPS_A + grp + 1) * A_WIDTH]
        v_new = qkv[:, (2 * N_GROUPS_A + grp) * A_WIDTH:(2 * N_GROUPS_A + grp + 1) * A_WIDTH]
        q_rep = jnp.concatenate(
            [jnp.broadcast_to(q[t:t + 1], (N_SLOTS, A_WIDTH)) for t in range(DEC_SEQ)], axis=0)
        q_bd = jnp.where(own_head, q_rep, 0.0).astype(BF16)
        k_new = jnp.concatenate([k_new, pad], axis=0).astype(BF16)
        v_new = jnp.concatenate([v_new, pad], axis=0).astype(BF16)
        s_c = jnp.where(reach(window, -window, d), _dot(q_bd, c_ref[0].astype(BF16)), NEG)
        s_n = jnp.where(reach(NEW_PAD, 0, d), _dot_nt(q_bd, k_new), NEG)
        mx = jnp.maximum(jnp.max(s_c, axis=-1, keepdims=True), jnp.max(s_n, axis=-1, keepdims=True))
        p_c = jnp.exp(s_c - mx)
        p_n = jnp.exp(s_n - mx)
        den = jnp.sum(p_c, axis=-1, keepdims=True) + jnp.sum(p_n, axis=-1, keepdims=True)
        o = (_dot_nt(p_c.astype(BF16), c_ref[1].astype(BF16)) + _dot(p_n.astype(BF16), v_new)) / den
        lse = jnp.broadcast_to(mx + jnp.log(den), (n_rows, A_WIDTH))
        o = jnp.where(own_head, o, 0.0)
        lse = jnp.where(own_head, lse, 0.0)
        outs.append(jnp.concatenate(
            [jnp.sum(o[t * N_SLOTS:(t + 1) * N_SLOTS], axis=0, keepdims=True) for t in range(DEC_SEQ)],
            axis=0))
        lses.append(jnp.concatenate(
            [jnp.sum(lse[t * N_SLOTS:(t + 1) * N_SLOTS], axis=0, keepdims=True) for t in range(DEC_SEQ)],
            axis=0))
    y_ref[...] = _merge(outs, lses)


def _sample_attention(qkv, caches, li):
    views = [jnp.transpose(c, (0, 1, 3, 4, 5, 2)).reshape(-1, DEC_BATCH, 2, A_WIDTH, c.shape[2])
             for c in caches]
    return pl.pallas_call(
        _sample_attn_kernel,
        grid=(DEC_BATCH,),
        in_specs=[pl.BlockSpec((None, DEC_SEQ, QKV_WIDTH), lambda b: (b, 0, 0))] + [
            pl.BlockSpec((None, None, 2, A_WIDTH, w), lambda b: (li, b, 0, 0, 0)) for w in DIL_WINDOWS],
        out_specs=pl.BlockSpec((None, DEC_SEQ, A_WIDTH), lambda b: (b, 0, 0)),
        out_shape=jax.ShapeDtypeStruct((DEC_BATCH, DEC_SEQ, A_WIDTH), F32),
        compiler_params=_params(1),
        name="sample_attention",
    )(qkv.reshape(DEC_BATCH, DEC_SEQ, QKV_WIDTH), *views)


def _proj_kernel(x_ref, y_ref, w_ref, out_ref):
    out_ref[...] = x_ref[...] + _dot(y_ref[...].astype(BF16), w_ref[...])


def _proj(x, y, layer, w):
    m = x.shape[0]
    return pl.pallas_call(
        _proj_kernel,
        grid=(1,),
        in_specs=[pl.BlockSpec((m, D_MODEL), lambda i: (0, 0)),
                  pl.BlockSpec((m, A_WIDTH), lambda i: (0, 0)),
                  pl.BlockSpec((None, A_WIDTH, D_MODEL), lambda i: (layer // 2, 0, 0))],
        out_specs=pl.BlockSpec((m, D_MODEL), lambda i: (0, 0)),
        out_shape=jax.ShapeDtypeStruct((m, D_MODEL), F32),
        compiler_params=_params(1),
        name="proj",
    )(x, y, w)


def _gelu(x):
    a0 = -2.0 * (2.0 / jnp.pi) ** 0.5
    return x / (1.0 + jnp.exp(x * (a0 + (a0 * 0.044715) * (x * x))))


def _gmlp_kernel(x_ref, g_ref, wuv_ref, lng_ref, lnb_ref, ws_ref, bs_ref, wo_ref, *rest,
                 tm, sample):
    if sample:
        out_ref, v_ref, zv_ref, vn_ref, um_ref = rest
    else:
        out_ref, zv_ref, vn_ref, um_ref = rest
    x = x_ref[...]
    h = _rms(x, g_ref[...]).astype(BF16)
    n_uv = D_V // UV_COLS

    tot = jnp.zeros((tm, 1), F32)
    for c in range(n_uv):
        lo = c * UV_COLS
        z = _gelu(_dot(h, wuv_ref[:, D_V + lo:D_V + lo + UV_COLS]))
        zv_ref[:, lo:lo + UV_COLS] = z
        tot = tot + jnp.sum(z, axis=-1, keepdims=True)
    mu = tot / D_V
    sq = jnp.zeros((tm, 1), F32)
    for c in range(n_uv):
        zc = zv_ref[:, c * UV_COLS:(c + 1) * UV_COLS] - mu
        sq = sq + jnp.sum(zc * zc, axis=-1, keepdims=True)
    rstd = lax.rsqrt(sq / D_V + LN_EPS)
    for c in range(n_uv):
        cols = slice(c * UV_COLS, (c + 1) * UV_COLS)
        vn = (zv_ref[:, cols] - mu) * rstd * lng_ref[:, cols] + lnb_ref[:, cols]
        vn_ref[:, cols] = vn.astype(BF16)
        if sample:
            v_ref[:, cols] = vn

    ri = lax.broadcasted_iota(jnp.int32, (CHUNK, CHUNK), 0)
    ci = lax.broadcasted_iota(jnp.int32, (CHUNK, CHUNK), 1)
    causal = ci <= ri
    if sample:
        causal = causal & ((ri // DEC_SEQ) == (ci // DEC_SEQ))
    groups_per_mm = UV_COLS // GROUP_B
    for c in range(n_uv):
        u = _gelu(_dot(h, wuv_ref[:, c * UV_COLS:(c + 1) * UV_COLS]))
        for gl in range(groups_per_mm):
            grp = c * groups_per_mm + gl
            w = jnp.where(causal, ws_ref[grp], 0.0).astype(BF16)
            bias = bs_ref[:, grp:grp + 1]
            cols = slice(grp * GROUP_B, (grp + 1) * GROUP_B)
            for n in range(tm // CHUNK):
                rows = slice(n * CHUNK, (n + 1) * CHUNK)
                mixed = _dot(w, vn_ref[rows, cols]) + bias
                um_ref[rows, cols] = (u[rows, gl * GROUP_B:(gl + 1) * GROUP_B] * mixed).astype(BF16)
    out_ref[...] = x + _dot(um_ref[...], wo_ref[...])


def _gmlp(x, layer, g, w_uv, ln_g, ln_b, w_s, b_s, w_out, *, tm, sample):
    m = x.shape[0]
    li = layer // 2
    out_specs = [pl.BlockSpec((tm, D_MODEL), lambda i: (i, 0))]
    out_shape = [jax.ShapeDtypeStruct((m, D_MODEL), F32)]
    if sample:
        out_specs.append(pl.BlockSpec((tm, D_V), lambda i: (i, 0)))
        out_shape.append(jax.ShapeDtypeStruct((m, D_V), F32))
    return pl.pallas_call(
        functools.partial(_gmlp_kernel, tm=tm, sample=sample),
        grid=(m // tm,),
        in_specs=[
            pl.BlockSpec((tm, D_MODEL), lambda i: (i, 0)),
            pl.BlockSpec((None, 1, D_MODEL), lambda i: (layer, 0, 0)),
            _resident((None, D_MODEL, 2 * D_V), lambda i: (li, 0, 0)),
            pl.BlockSpec((None, 1, D_V), lambda i: (li, 0, 0)),
            pl.BlockSpec((None, 1, D_V), lambda i: (li, 0, 0)),
            pl.BlockSpec((None, N_GROUPS_B, CHUNK, CHUNK), lambda i: (li, 0, 0, 0)),
            pl.BlockSpec((None, CHUNK, N_GROUPS_B), lambda i: (li, 0, 0)),
            _resident((None, D_V, D_MODEL), lambda i: (li, 0, 0)),
        ],
        out_specs=out_specs,
        out_shape=out_shape,
        scratch_shapes=[pltpu.VMEM((tm, D_V), F32), pltpu.VMEM((tm, D_V), BF16),
                        pltpu.VMEM((tm, D_V), BF16)],
        compiler_params=_params(1),
        name="gmlp",
    )(x, g, w_uv, ln_g, ln_b, w_s, b_s, w_out)


def kernel(x_prompt, x_sample, cache_kv_w128, cache_kv_w512, cache_kv_w2048, norm_ffn1, w_ffn1_in,
           w_ffn1_out, norm_mix, norm_ffn2, w_ffn2_in, w_ffn2_out, w_qkv_a, w_out_a, w_uv_b,
           ln_v_gain, ln_v_bias, w_spatial, b_spatial, w_out_b, norm_final):
    caches = (cache_kv_w128, cache_kv_w512, cache_kv_w2048)
    mp, ms = BATCH * SEQ, DEC_BATCH * DEC_SEQ
    xp = x_prompt.reshape(mp, D_MODEL)
    xs = x_sample.reshape(ms, D_MODEL)

    w1i, w1o, w2i, w2o = (w.astype(BF16) for w in (w_ffn1_in, w_ffn1_out, w_ffn2_in, w_ffn2_out))
    wqkv, woa, wuv, wob = (w.astype(BF16) for w in (w_qkv_a, w_out_a, w_uv_b, w_out_b))
    g1 = norm_ffn1.reshape(DEPTH, 1, D_MODEL)
    gm = norm_mix.reshape(DEPTH, 1, D_MODEL)
    g2 = norm_ffn2.reshape(DEPTH, 1, D_MODEL)
    gf = norm_final.reshape(1, D_MODEL)
    lng = ln_v_gain.reshape(-1, 1, D_V)
    lnb = ln_v_bias.reshape(-1, 1, D_V)
    reps = CHUNK // DEC_SEQ
    ws_p = w_spatial
    bs_p = jnp.swapaxes(b_spatial, 1, 2)
    ws_s = jnp.tile(w_spatial[:, :, :DEC_SEQ, :DEC_SEQ], (1, 1, reps, reps))
    bs_s = jnp.swapaxes(jnp.tile(b_spatial[:, :, :DEC_SEQ], (1, 1, reps)), 1, 2)

    win_p = [[] for _ in range(N_GROUPS_A)]
    kv_s, v_rows = [], []
    for i in range(DEPTH):
        li = i // 2
        xp = _ffn(xp, i, g1, w1i, w1o, gf, tm=TILE)
        xs = _ffn(xs, i, g1, w1i, w1o, gf, tm=ms)
        if i % 2 == 0:
            *qkv_g, t0, t1, t2 = _qkv_prompt(xp, i, gm, wqkv)
            for grp, t in enumerate((t0, t1, t2)):
                win_p[grp].append(t)
            parts = [_band_attention(qkv_g[grp], grp) for grp in range(N_GROUPS_A)]
            xp = _merge_proj(xp, [p[0] for p in parts], [p[1] for p in parts], i, woa)
            qkv, kv = _qkv_sample(xs, i, gm, wqkv)
            kv_s.append(kv)
            y = _sample_attention(qkv, caches, li)
            xs = _proj(xs, y.reshape(ms, A_WIDTH), i, woa)
        else:
            (xp,) = _gmlp(xp, i, gm, wuv, lng, lnb, ws_p, bs_p, wob, tm=TILE, sample=False)
            xs, v = _gmlp(xs, i, gm, wuv, lng, lnb, ws_s, bs_s, wob, tm=ms, sample=True)
            v_rows.append(v)
        last = i == DEPTH - 1
        xp = _ffn(xp, i, g2, w2i, w2o, gf, tm=TILE, final=last)
        xs = _ffn(xs, i, g2, w2i, w2o, gf, tm=ms, final=last)

    def prompt_window(grp):
        return jnp.transpose(jnp.stack(win_p[grp]), (0, 1, 5, 2, 3, 4))

    def sample_rows(grp):
        return jnp.stack([kv.reshape(DEC_BATCH, DEC_SEQ, N_GROUPS_A, 2, N_SLOTS, HEAD_DIM)[:, :, grp]
                          for kv in kv_s])

    return (xp.reshape(BATCH, SEQ, D_MODEL), xs.reshape(DEC_BATCH, DEC_SEQ, D_MODEL),
            prompt_window(0), prompt_window(1), prompt_window(2),
            sample_rows(0), sample_rows(1), sample_rows(2),
            jnp.stack(v_rows).reshape(len(v_rows), DEC_BATCH, DEC_SEQ, D_V))
```

```python
import functools

import jax
import jax.numpy as jnp
from jax import lax
from jax.experimental import pallas as pl
from jax.experimental.pallas import tpu as pltpu

F32 = jnp.float32
BF16 = jnp.bfloat16

D_MODEL = 1024
BATCH = 4
SEQ = 4096
DEPTH = 4
DEC_BATCH = 32
DEC_SEQ = 4
HEAD_DIM = 64
N_SLOTS = 8
DIL_WINDOWS = (128, 512, 2048)
DIL_RATES = (1, 4, 16)
N_GROUPS_A = 3
A_WIDTH = N_SLOTS * HEAD_DIM
QKV_WIDTH = 3 * N_GROUPS_A * A_WIDTH
KV_WIDTH = 2 * N_GROUPS_A * A_WIDTH
BAND = 128
CHUNK = 128
D_V = 3072
N_GROUPS_B = 8
GROUP_B = D_V // N_GROUPS_B
D_FF = 2816
RMS_EPS = 1e-6
LN_EPS = 1e-5
NEG = -1e30

VMEM_LIMIT_BYTES = 56 * 1024 * 1024
LANES = 128
BF16_ROWS = 16
TILE = 512
TILES_PER_SEQ = SEQ // TILE
MM_COLS = 256
FF_COLS = MM_COLS
UV_COLS = 768
Q_ROWS = 256
NEW_PAD = 16


def _params(n_axes):
    return pltpu.CompilerParams(dimension_semantics=("arbitrary",) * n_axes,
                                vmem_limit_bytes=VMEM_LIMIT_BYTES)


def _resident(shape, index_map):
    return pl.BlockSpec(shape, index_map, pipeline_mode=pl.Buffered(1))


def _rms(x, g):
    return x * lax.rsqrt(jnp.mean(x * x, axis=-1, keepdims=True) + RMS_EPS) * g


def _dot(a, b):
    return jnp.dot(a, b, preferred_element_type=F32)


def _dot_nt(a, b):
    return lax.dot_general(a, b, (((1,), (1,)), ((), ())), preferred_element_type=F32)


def _cast_plan(src, layer, n_steps):
    rows, cols = src.shape[1:]
    rb = rows // n_steps
    if rows % n_steps or rb % BF16_ROWS:
        rb = LANES
    nb = rows // rb
    assert rows % rb == 0 and nb <= n_steps
    in_spec = pl.BlockSpec((None, rb, cols), lambda i: (layer, jnp.minimum(i, nb - 1), 0))
    out_spec = pl.BlockSpec((rb, cols), lambda i: (jnp.minimum(i, nb - 1), 0))
    return in_spec, out_spec, jax.ShapeDtypeStruct((rows, cols), BF16)


def _cast_rows(src_refs, dst_refs):
    for src, dst in zip(src_refs, dst_refs, strict=True):
        dst[...] = src[...].astype(BF16)


def _merge(outs, lses):
    mx = jnp.maximum(jnp.maximum(lses[0], lses[1]), lses[2])
    e = [jnp.exp(l - mx) for l in lses]
    den = e[0] + e[1] + e[2]
    return (e[0] / den) * outs[0] + (e[1] / den) * outs[1] + (e[2] / den) * outs[2]


def _position_order(slab_ref, src_ref, d):
    n = TILE // d
    for r in range(d):
        for cc in range(A_WIDTH // LANES):
            slab_ref[cc, pl.ds(r, n, stride=d), :] = src_ref[r * n:(r + 1) * n,
                                                             cc * LANES:(cc + 1) * LANES]
    return [slab_ref[cc] for cc in range(A_WIDTH // LANES)]


def _merged_attention(o0, o1, o2, l0, l1, l2, slab_ref):
    cols = [slice(cc * LANES, (cc + 1) * LANES) for cc in range(A_WIDTH // LANES)]
    outs = [[o0[:, c] for c in cols],
            _position_order(slab_ref.at[0], o1, DIL_RATES[1]),
            _position_order(slab_ref.at[1], o2, DIL_RATES[2])]
    lses = [[l0[:, c] for c in cols],
            _position_order(slab_ref.at[2], l1, DIL_RATES[1]),
            _position_order(slab_ref.at[3], l2, DIL_RATES[2])]
    return jnp.concatenate(
        [_merge([o[cc] for o in outs], [l[cc] for l in lses]) for cc in range(len(cols))], axis=1)


def _ffn_kernel(*refs, final, mixer, n_cast):
    x_ref, g_ref, win_ref, wout_ref, gf_ref = refs[:5]
    refs = refs[5:]
    if mixer:
        *attn_refs, wmix_ref = refs[:7]
        refs = refs[7:]
    cast_src, o_ref, cast_dst = refs[:n_cast], refs[n_cast], refs[n_cast + 1:2 * n_cast + 1]
    scratch = refs[2 * n_cast + 1:]
    a_ref = scratch[0]
    _cast_rows(cast_src, cast_dst)
    x = x_ref[...]
    if mixer:
        x = x + _dot(_merged_attention(*attn_refs, scratch[1]).astype(BF16), wmix_ref[...])
    h = _rms(x, g_ref[...]).astype(BF16)
    for c in range(D_FF // FF_COLS):
        lo = c * FF_COLS
        gate = _dot(h, win_ref[:, lo:lo + FF_COLS])
        up = _dot(h, win_ref[:, D_FF + lo:D_FF + lo + FF_COLS])
        a_ref[:, lo:lo + FF_COLS] = (gate * jax.nn.sigmoid(gate) * up).astype(BF16)
    y = x + 0.5 * _dot(a_ref[...], wout_ref[...])
    if final:
        y = _rms(y, gf_ref[...])
    o_ref[...] = y


def _ffn(x, layer, g, w_in, w_out, g_final, *, tm, final=False, mixer=None, casts=()):
    m = x.shape[0]
    steps = m // tm
    plans = [_cast_plan(src, lyr, steps) for src, lyr in casts]
    in_specs = [
        pl.BlockSpec((tm, D_MODEL), lambda i: (i, 0)),
        pl.BlockSpec((None, 1, D_MODEL), lambda i: (layer, 0, 0)),
        _resident((D_MODEL, 2 * D_FF), lambda i: (0, 0)),
        _resident((D_FF, D_MODEL), lambda i: (0, 0)),
        pl.BlockSpec((1, D_MODEL), lambda i: (0, 0)),
    ]
    args = [x, g, w_in, w_out, g_final]
    scratch = [pltpu.VMEM((tm, D_FF), BF16)]
    if mixer is not None:
        outs, lses, w_mix = mixer
        in_specs += [pl.BlockSpec((tm, A_WIDTH), lambda i: (i, 0))] * 6
        in_specs.append(_resident((A_WIDTH, D_MODEL), lambda i: (0, 0)))
        args += [*outs, *lses, w_mix]
        scratch.append(pltpu.VMEM((4, A_WIDTH // LANES, TILE, LANES), F32))
    res = pl.pallas_call(
        functools.partial(_ffn_kernel, final=final, mixer=mixer is not None, n_cast=len(plans)),
        grid=(steps,),
        in_specs=in_specs + [p[0] for p in plans],
        out_specs=[pl.BlockSpec((tm, D_MODEL), lambda i: (i, 0))] + [p[1] for p in plans],
        out_shape=[jax.ShapeDtypeStruct((m, D_MODEL), F32)] + [p[2] for p in plans],
        scratch_shapes=scratch,
        compiler_params=_params(1),
        name="ffn",
    )(*args, *[src for src, _ in casts])
    return res[0], res[1:]


def _qkv_sample_kernel(x_ref, g_ref, w_ref, qkv_ref, kv_ref):
    h = _rms(x_ref[...], g_ref[...]).astype(BF16)
    for c in range(QKV_WIDTH // A_WIDTH):
        lo = c * A_WIDTH
        part, grp = divmod(c, N_GROUPS_A)
        y = _dot(h, w_ref[:, lo:lo + A_WIDTH])
        if part == 0:
            qkv_ref[:, lo:lo + A_WIDTH] = (y * (HEAD_DIM ** -0.5)).astype(BF16)
        else:
            qkv_ref[:, lo:lo + A_WIDTH] = y.astype(BF16)
            dst = (2 * grp + part - 1) * A_WIDTH
            kv_ref[:, dst:dst + A_WIDTH] = y


def _qkv_sample(x, layer, g, w):
    m = x.shape[0]
    return pl.pallas_call(
        _qkv_sample_kernel,
        grid=(1,),
        in_specs=[
            pl.BlockSpec((m, D_MODEL), lambda i: (0, 0)),
            pl.BlockSpec((None, 1, D_MODEL), lambda i: (layer, 0, 0)),
            pl.BlockSpec((D_MODEL, QKV_WIDTH), lambda i: (0, 0)),
        ],
        out_specs=[
            pl.BlockSpec((m, QKV_WIDTH), lambda i: (0, 0)),
            pl.BlockSpec((m, KV_WIDTH), lambda i: (0, 0)),
        ],
        out_shape=[jax.ShapeDtypeStruct((m, QKV_WIDTH), BF16),
                   jax.ShapeDtypeStruct((m, KV_WIDTH), F32)],
        compiler_params=_params(1),
        name="qkv_sample",
    )(x, g, w)


def _regroup_rows(slab_ref, y, dst_ref, col0, d):
    n = TILE // d
    for cc in range(A_WIDTH // LANES):
        slab_ref[cc] = y[:, cc * LANES:(cc + 1) * LANES]
    for r in range(d):
        for cc in range(A_WIDTH // LANES):
            dst_ref[r * n:(r + 1) * n, col0 + cc * LANES:col0 + (cc + 1) * LANES] = (
                slab_ref[cc, pl.ds(r, n, stride=d), :].astype(BF16))


def _qkv_prompt_kernel(*refs, n_cast, n_prev):
    x_ref, g_ref, w_ref = refs[:3]
    cast_src = refs[3 + n_prev:3 + n_prev + n_cast]
    outs = refs[3 + n_prev + n_cast:]
    dst, win = outs[:N_GROUPS_A], outs[N_GROUPS_A:2 * N_GROUPS_A]
    cast_dst = outs[2 * N_GROUPS_A:2 * N_GROUPS_A + n_cast]
    slab_ref = outs[2 * N_GROUPS_A + n_cast]
    _cast_rows(cast_src, cast_dst)
    h = _rms(x_ref[...], g_ref[...]).astype(BF16)
    for c in range(QKV_WIDTH // A_WIDTH):
        part, grp = divmod(c, N_GROUPS_A)
        y = _dot(h, w_ref[:, c * A_WIDTH:(c + 1) * A_WIDTH])
        if part == 0:
            y = y * (HEAD_DIM ** -0.5)
        col0 = part * A_WIDTH
        if grp == 0:
            dst[0][:, col0:col0 + A_WIDTH] = y.astype(BF16)
        else:
            _regroup_rows(slab_ref.at[(grp - 1) * 3 + part], y, dst[grp], col0, DIL_RATES[grp])
        if part > 0:
            keep = min(DIL_WINDOWS[grp], TILE)
            win[grp][part - 1] = y[TILE - keep:].T.reshape(N_SLOTS, HEAD_DIM, keep)


def _qkv_prompt(x, layer, g, w, windows=None, casts=()):
    m = x.shape[0]
    li = layer // 2
    steps = m // TILE
    plans = [_cast_plan(src, lyr, steps) for src, lyr in casts]
    row_spec = pl.BlockSpec((TILE, 3 * A_WIDTH), lambda i: (i, 0))
    win_specs, win_shapes = [], []
    for grp in range(N_GROUPS_A):
        keep = min(DIL_WINDOWS[grp], TILE)
        first_tile = TILES_PER_SEQ - max(DIL_WINDOWS[grp] // TILE, 1)
        win_specs.append(pl.BlockSpec(
            (None, None, 2, N_SLOTS, HEAD_DIM, keep),
            lambda i, ft=first_tile: (li, i // TILES_PER_SEQ, 0, 0, 0,
                                      jnp.maximum(i % TILES_PER_SEQ - ft, 0))))
        win_shapes.append(jax.ShapeDtypeStruct(
            (DEPTH // 2, BATCH, 2, N_SLOTS, HEAD_DIM, DIL_WINDOWS[grp]), F32))
    prev = list(windows) if windows is not None else []
    n_fixed = 3
    res = pl.pallas_call(
        functools.partial(_qkv_prompt_kernel, n_cast=len(plans), n_prev=len(prev)),
        grid=(steps,),
        in_specs=[
            pl.BlockSpec((TILE, D_MODEL), lambda i: (i, 0)),
            pl.BlockSpec((None, 1, D_MODEL), lambda i: (layer, 0, 0)),
            _resident((D_MODEL, QKV_WIDTH), lambda i: (0, 0)),
        ] + [pl.BlockSpec(memory_space=pl.ANY)] * len(prev) + [p[0] for p in plans],
        out_specs=[row_spec] * N_GROUPS_A + win_specs + [p[1] for p in plans],
        out_shape=([jax.ShapeDtypeStruct((m, 3 * A_WIDTH), BF16)] * N_GROUPS_A + win_shapes
                   + [p[2] for p in plans]),
        input_output_aliases={n_fixed + k: N_GROUPS_A + k for k in range(len(prev))},
        scratch_shapes=[pltpu.VMEM((6, A_WIDTH // LANES, TILE, LANES), F32)],
        compiler_params=_params(1),
        name="qkv_prompt",
    )(x, g, w, *prev, *[src for src, _ in casts])
    return res[:N_GROUPS_A], res[N_GROUPS_A:2 * N_GROUPS_A], res[2 * N_GROUPS_A:]


def _band_kernel(q_ref, kp_ref, ko_ref, vp_ref, vo_ref, o_ref, l_ref, s_ref, p_ref, m_ref):
    n = q_ref.shape[1]
    n_sub, n_pair = Q_ROWS // BAND, A_WIDTH // LANES
    first_key = jnp.where(pl.program_id(2) == 0, BAND, 0)
    qi = lax.broadcasted_iota(jnp.int32, (2 * BAND, 2 * BAND), 0) % BAND
    ki = lax.broadcasted_iota(jnp.int32, (2 * BAND, 2 * BAND), 1)
    band = (ki >= qi) & (ki <= qi + BAND)
    bias = jnp.where(band, 0.0, NEG)
    bias_first = jnp.where(band & (ki >= first_key), 0.0, NEG)
    low = lax.broadcasted_iota(jnp.int32, (1, LANES), 1) < HEAD_DIM
    q_all = q_ref[...].reshape(Q_ROWS, A_WIDTH)
    ko_all = ko_ref[...].reshape(Q_ROWS, A_WIDTH)
    vo_all = vo_ref[...].reshape(Q_ROWS, A_WIDTH)
    kp_all = kp_ref[...].reshape(BAND, A_WIDTH)
    vp_all = vp_ref[...].reshape(BAND, A_WIDTH)

    def keys(prev, own, j, cs):
        if j == 0:
            return jnp.concatenate([prev[:, cs], own[:BAND, cs]], axis=0)
        return own[:, cs]

    for pr in range(n_pair):
        cs = slice(pr * LANES, (pr + 1) * LANES)
        for j in range(n_sub):
            q = q_all[j * BAND:(j + 1) * BAND, cs]
            zero = jnp.zeros_like(q)
            q_ab = jnp.concatenate([jnp.where(low, q, zero), jnp.where(low, zero, q)], axis=0)
            s = _dot_nt(q_ab, keys(kp_all, ko_all, j, cs))
            s_ref[pr * n_sub + j] = s + (bias_first if j == 0 else bias)

    for u in range(n_pair * n_sub):
        mx = jnp.max(s_ref[u], axis=-1, keepdims=True)
        p_ref[u] = jnp.exp(s_ref[u] - mx).astype(BF16)
        m_ref[u] = jnp.where(low, mx[:BAND], mx[BAND:])

    one = jnp.ones((2 * BAND, LANES), BF16)
    for pr in range(n_pair):
        cs = slice(pr * LANES, (pr + 1) * LANES)
        for j in range(n_sub):
            u = pr * n_sub + j
            vv = keys(vp_all, vo_all, j, cs)
            oa = _dot(p_ref[u, :BAND], jnp.where(low, vv, one))
            ob = _dot(p_ref[u, BAND:], jnp.where(low, one, vv))
            den = pltpu.roll(jnp.where(low, ob, oa), HEAD_DIM, axis=1)
            lse = m_ref[u] + jnp.log(den)
            tiles = slice(j * (BAND // n), (j + 1) * (BAND // n))
            o_ref[tiles, :, cs] = (jnp.where(low, oa, ob) / den).reshape(BAND // n, n, LANES)
            l_ref[tiles, :, cs] = lse.reshape(BAND // n, n, LANES)


def _band_attention(qkv, grp):
    d = DIL_RATES[grp]
    n = min(TILE // d, BAND)
    pieces = BATCH * SEQ // (d * n)
    nb = SEQ // (d * Q_ROWS)
    units = (Q_ROWS // BAND) * (A_WIDTH // LANES)
    view = qkv.reshape(pieces, d, n, 3 * A_WIDTH)

    def own(part):
        return pl.BlockSpec((Q_ROWS // n, None, n, A_WIDTH), lambda b, r, i: (b * nb + i, r, 0, part))

    def prev(part):
        return pl.BlockSpec((BAND // n, None, n, A_WIDTH),
                            lambda b, r, i: (jnp.maximum((b * nb + i) * (Q_ROWS // BAND) - 1, 0),
                                             r, 0, part))

    out_spec = pl.BlockSpec((Q_ROWS // n, None, n, A_WIDTH), lambda b, r, i: (b * nb + i, r, 0, 0))
    out_shape = jax.ShapeDtypeStruct((pieces, d, n, A_WIDTH), F32)
    o, lse = pl.pallas_call(
        _band_kernel,
        grid=(BATCH, d, nb),
        in_specs=[own(0), prev(1), own(1), prev(2), own(2)],
        out_specs=[out_spec, out_spec],
        out_shape=[out_shape, out_shape],
        scratch_shapes=[pltpu.VMEM((units, 2 * BAND, 2 * BAND), F32),
                        pltpu.VMEM((units, 2 * BAND, 2 * BAND), BF16),
                        pltpu.VMEM((units, BAND, LANES), F32)],
        compiler_params=_params(3),
        name=f"band_attention_d{d}",
    )(view, view, view, view, view)
    return o.reshape(BATCH * SEQ, A_WIDTH), lse.reshape(BATCH * SEQ, A_WIDTH)


def _sample_attn_kernel(qkv_ref, c0_ref, c1_ref, c2_ref, y_ref):
    n_rows = DEC_SEQ * N_SLOTS
    qkv = qkv_ref[...].astype(F32)
    row_h = lax.broadcasted_iota(jnp.int32, (n_rows, A_WIDTH), 0) % N_SLOTS
    col_h = lax.broadcasted_iota(jnp.int32, (n_rows, A_WIDTH), 1) // HEAD_DIM
    own_head = row_h == col_h
    pad = jnp.zeros((NEW_PAD - DEC_SEQ, A_WIDTH), F32)

    def reach(n_keys, offset, d):
        t = lax.broadcasted_iota(jnp.int32, (n_rows, n_keys), 0) // N_SLOTS
        back = t - lax.broadcasted_iota(jnp.int32, (n_rows, n_keys), 1) - offset
        return (back >= 0) & (back <= BAND * d) & ((back & (d - 1)) == 0)

    outs, lses = [], []
    for grp, c_ref in enumerate((c0_ref, c1_ref, c2_ref)):
        d, window = DIL_RATES[grp], DIL_WINDOWS[grp]
        q = qkv[:, grp * A_WIDTH:(grp + 1) * A_WIDTH]
        k_new = qkv[:, (N_GROUPS_A + grp) * A_WIDTH:(N_GROUPS_A + grp + 1) * A_WIDTH]
        v_new = qkv[:, (2 * N_GROUPS_A + grp) * A_WIDTH:(2 * N_GROUPS_A + grp + 1) * A_WIDTH]
        q_rep = jnp.concatenate(
            [jnp.broadcast_to(q[t:t + 1], (N_SLOTS, A_WIDTH)) for t in range(DEC_SEQ)], axis=0)
        q_bd = jnp.where(own_head, q_rep, 0.0).astype(BF16)
        k_new = jnp.concatenate([k_new, pad], axis=0).astype(BF16)
        v_new = jnp.concatenate([v_new, pad], axis=0).astype(BF16)
        s_c = jnp.where(reach(window, -window, d), _dot(q_bd, c_ref[0].astype(BF16)), NEG)
        s_n = jnp.where(reach(NEW_PAD, 0, d), _dot_nt(q_bd, k_new), NEG)
        mx = jnp.maximum(jnp.max(s_c, axis=-1, keepdims=True), jnp.max(s_n, axis=-1, keepdims=True))
        p_c = jnp.exp(s_c - mx)
        p_n = jnp.exp(s_n - mx)
        den = jnp.sum(p_c, axis=-1, keepdims=True) + jnp.sum(p_n, axis=-1, keepdims=True)
        o = (_dot_nt(p_c.astype(BF16), c_ref[1].astype(BF16)) + _dot(p_n.astype(BF16), v_new)) / den
        lse = jnp.broadcast_to(mx + jnp.log(den), (n_rows, A_WIDTH))
        o = jnp.where(own_head, o, 0.0)
        lse = jnp.where(own_head, lse, 0.0)
        outs.append(jnp.concatenate(
            [jnp.sum(o[t * N_SLOTS:(t + 1) * N_SLOTS], axis=0, keepdims=True) for t in range(DEC_SEQ)],
            axis=0))
        lses.append(jnp.concatenate(
            [jnp.sum(lse[t * N_SLOTS:(t + 1) * N_SLOTS], axis=0, keepdims=True) for t in range(DEC_SEQ)],
            axis=0))
    y_ref[...] = _merge(outs, lses)


def _sample_attention(qkv, caches, li):
    views = [jnp.transpose(c, (0, 1, 3, 4, 5, 2)).reshape(-1, DEC_BATCH, 2, A_WIDTH, c.shape[2])
             for c in caches]
    return pl.pallas_call(
        _sample_attn_kernel,
        grid=(DEC_BATCH,),
        in_specs=[pl.BlockSpec((None, DEC_SEQ, QKV_WIDTH), lambda b: (b, 0, 0))] + [
            pl.BlockSpec((None, None, 2, A_WIDTH, w), lambda b: (li, b, 0, 0, 0)) for w in DIL_WINDOWS],
        out_specs=pl.BlockSpec((None, DEC_SEQ, A_WIDTH), lambda b: (b, 0, 0)),
        out_shape=jax.ShapeDtypeStruct((DEC_BATCH, DEC_SEQ, A_WIDTH), F32),
        compiler_params=_params(1),
        name="sample_attention",
    )(qkv.reshape(DEC_BATCH, DEC_SEQ, QKV_WIDTH), *views)


def _proj_kernel(x_ref, y_ref, w_ref, out_ref):
    out_ref[...] = x_ref[...] + _dot(y_ref[...].astype(BF16), w_ref[...])


def _proj(x, y, w):
    m = x.shape[0]
    return pl.pallas_call(
        _proj_kernel,
        grid=(1,),
        in_specs=[pl.BlockSpec((m, D_MODEL), lambda i: (0, 0)),
                  pl.BlockSpec((m, A_WIDTH), lambda i: (0, 0)),
                  pl.BlockSpec((A_WIDTH, D_MODEL), lambda i: (0, 0))],
        out_specs=pl.BlockSpec((m, D_MODEL), lambda i: (0, 0)),
        out_shape=jax.ShapeDtypeStruct((m, D_MODEL), F32),
        compiler_params=_params(1),
        name="proj",
    )(x, y, w)


def _gelu(x):
    a0 = -2.0 * (2.0 / jnp.pi) ** 0.5
    return x / (1.0 + jnp.exp(x * (a0 + (a0 * 0.044715) * (x * x))))


def _gmlp_kernel(*refs, tm, sample, n_cast):
    x_ref, g_ref, wuv_ref, lng_ref, lnb_ref, ws_ref, bs_ref, wo_ref = refs[:8]
    cast_src = refs[8:8 + n_cast]
    refs = refs[8 + n_cast:]
    out_ref = refs[0]
    if sample:
        v_ref = refs[1]
    refs = refs[2:] if sample else refs[1:]
    cast_dst = refs[:n_cast]
    zv_ref, vn_ref, um_ref = refs[n_cast:]
    _cast_rows(cast_src, cast_dst)
    x = x_ref[...]
    h = _rms(x, g_ref[...]).astype(BF16)
    n_uv = D_V // UV_COLS

    tot = jnp.zeros((tm, 1), F32)
    for c in range(n_uv):
        lo = c * UV_COLS
        z = _gelu(_dot(h, wuv_ref[:, D_V + lo:D_V + lo + UV_COLS]))
        zv_ref[:, lo:lo + UV_COLS] = z
        tot = tot + jnp.sum(z, axis=-1, keepdims=True)
    mu = tot / D_V
    sq = jnp.zeros((tm, 1), F32)
    for c in range(n_uv):
        zc = zv_ref[:, c * UV_COLS:(c + 1) * UV_COLS] - mu
        sq = sq + jnp.sum(zc * zc, axis=-1, keepdims=True)
    rstd = lax.rsqrt(sq / D_V + LN_EPS)
    for c in range(n_uv):
        cols = slice(c * UV_COLS, (c + 1) * UV_COLS)
        vn = (zv_ref[:, cols] - mu) * rstd * lng_ref[:, cols] + lnb_ref[:, cols]
        vn_ref[:, cols] = vn.astype(BF16)
        if sample:
            v_ref[:, cols] = vn

    ri = lax.broadcasted_iota(jnp.int32, (CHUNK, CHUNK), 0)
    ci = lax.broadcasted_iota(jnp.int32, (CHUNK, CHUNK), 1)
    causal = ci <= ri
    if sample:
        causal = causal & ((ri // DEC_SEQ) == (ci // DEC_SEQ))
    groups_per_mm = UV_COLS // GROUP_B
    for c in range(n_uv):
        u = _gelu(_dot(h, wuv_ref[:, c * UV_COLS:(c + 1) * UV_COLS]))
        for gl in range(groups_per_mm):
            grp = c * groups_per_mm + gl
            w = jnp.where(causal, ws_ref[grp], 0.0).astype(BF16)
            bias = bs_ref[:, grp:grp + 1]
            cols = slice(grp * GROUP_B, (grp + 1) * GROUP_B)
            for n in range(tm // CHUNK):
                rows = slice(n * CHUNK, (n + 1) * CHUNK)
                mixed = _dot(w, vn_ref[rows, cols]) + bias
                um_ref[rows, cols] = (u[rows, gl * GROUP_B:(gl + 1) * GROUP_B] * mixed).astype(BF16)
    out_ref[...] = x + _dot(um_ref[...], wo_ref[...])


def _gmlp(x, layer, g, w_uv, ln_g, ln_b, w_s, b_s, w_out, *, tm, sample, casts=()):
    m = x.shape[0]
    li = layer // 2
    steps = m // tm
    plans = [_cast_plan(src, lyr, steps) for src, lyr in casts]
    out_specs = [pl.BlockSpec((tm, D_MODEL), lambda i: (i, 0))]
    out_shape = [jax.ShapeDtypeStruct((m, D_MODEL), F32)]
    if sample:
        out_specs.append(pl.BlockSpec((tm, D_V), lambda i: (i, 0)))
        out_shape.append(jax.ShapeDtypeStruct((m, D_V), F32))
    n_main = len(out_specs)
    res = pl.pallas_call(
        functools.partial(_gmlp_kernel, tm=tm, sample=sample, n_cast=len(plans)),
        grid=(steps,),
        in_specs=[
            pl.BlockSpec((tm, D_MODEL), lambda i: (i, 0)),
            pl.BlockSpec((None, 1, D_MODEL), lambda i: (layer, 0, 0)),
            _resident((D_MODEL, 2 * D_V), lambda i: (0, 0)),
            pl.BlockSpec((None, 1, D_V), lambda i: (li, 0, 0)),
            pl.BlockSpec((None, 1, D_V), lambda i: (li, 0, 0)),
            pl.BlockSpec((None, N_GROUPS_B, CHUNK, CHUNK), lambda i: (li, 0, 0, 0)),
            pl.BlockSpec((None, CHUNK, N_GROUPS_B), lambda i: (li, 0, 0)),
            _resident((D_V, D_MODEL), lambda i: (0, 0)),
        ] + [p[0] for p in plans],
        out_specs=out_specs + [p[1] for p in plans],
        out_shape=out_shape + [p[2] for p in plans],
        scratch_shapes=[pltpu.VMEM((tm, D_V), F32), pltpu.VMEM((tm, D_V), BF16),
                        pltpu.VMEM((tm, D_V), BF16)],
        compiler_params=_params(1),
        name="gmlp",
    )(x, g, w_uv, ln_g, ln_b, w_s, b_s, w_out, *[src for src, _ in casts])
    return res[:n_main], res[n_main:]


def kernel(x_prompt, x_sample, cache_kv_w128, cache_kv_w512, cache_kv_w2048, norm_ffn1, w_ffn1_in,
           w_ffn1_out, norm_mix, norm_ffn2, w_ffn2_in, w_ffn2_out, w_qkv_a, w_out_a, w_uv_b,
           ln_v_gain, ln_v_bias, w_spatial, b_spatial, w_out_b, norm_final):
    caches = (cache_kv_w128, cache_kv_w512, cache_kv_w2048)
    mp, ms = BATCH * SEQ, DEC_BATCH * DEC_SEQ
    xp = x_prompt.reshape(mp, D_MODEL)
    xs = x_sample.reshape(ms, D_MODEL)

    g1 = norm_ffn1.reshape(DEPTH, 1, D_MODEL)
    gm = norm_mix.reshape(DEPTH, 1, D_MODEL)
    g2 = norm_ffn2.reshape(DEPTH, 1, D_MODEL)
    gf = norm_final.reshape(1, D_MODEL)
    lng = ln_v_gain.reshape(-1, 1, D_V)
    lnb = ln_v_bias.reshape(-1, 1, D_V)
    reps = CHUNK // DEC_SEQ
    ws_p = w_spatial
    bs_p = jnp.swapaxes(b_spatial, 1, 2)
    ws_s = jnp.tile(w_spatial[:, :, :DEC_SEQ, :DEC_SEQ], (1, 1, reps, reps))
    bs_s = jnp.swapaxes(jnp.tile(b_spatial[:, :, :DEC_SEQ], (1, 1, reps)), 1, 2)

    f32_weights = {"ffn1_in": w_ffn1_in, "ffn1_out": w_ffn1_out, "ffn2_in": w_ffn2_in,
                   "ffn2_out": w_ffn2_out, "qkv": w_qkv_a, "out_a": w_out_a, "uv": w_uv_b,
                   "out_b": w_out_b}
    bf16_weights = {("ffn1_in", 0): w_ffn1_in[0].astype(BF16),
                    ("ffn1_out", 0): w_ffn1_out[0].astype(BF16)}

    def jobs(*keys):
        return keys, tuple((f32_weights[name], idx) for name, idx in keys)

    def done(keys, converted):
        bf16_weights.update(zip(keys, converted, strict=True))

    def w(name, idx):
        return bf16_weights[(name, idx)]

    windows, kv_s, v_rows = None, [], []
    for i in range(DEPTH):
        li = i // 2
        attention = i % 2 == 0
        last = i == DEPTH - 1

        if not attention:
            keys, casts = jobs(("uv", li), ("out_b", li), ("ffn2_in", i), ("ffn2_out", i))
        elif i == 0:
            keys, casts = jobs(("qkv", li), ("out_a", li), ("ffn2_in", i), ("ffn2_out", i))
        else:
            keys, casts = jobs(("ffn2_in", i), ("ffn2_out", i))
        xp, converted = _ffn(xp, i, g1, w("ffn1_in", i), w("ffn1_out", i), gf, tm=TILE, casts=casts)
        done(keys, converted)
        xs, _ = _ffn(xs, i, g1, w("ffn1_in", i), w("ffn1_out", i), gf, tm=ms)

        keys, casts = jobs() if last else jobs(("ffn1_in", i + 1), ("ffn1_out", i + 1))
        if attention:
            qkv_g, windows, converted = _qkv_prompt(xp, i, gm, w("qkv", li), windows, casts)
            done(keys, converted)
            parts = [_band_attention(qkv_g[grp], grp) for grp in range(N_GROUPS_A)]
            mixer = ([p[0] for p in parts], [p[1] for p in parts], w("out_a", li))
            xp, _ = _ffn(xp, i, g2, w("ffn2_in", i), w("ffn2_out", i), gf, tm=TILE, final=last,
                         mixer=mixer)
            qkv, kv = _qkv_sample(xs, i, gm, w("qkv", li))
            kv_s.append(kv)
            y = _sample_attention(qkv, caches, li)
            xs = _proj(xs, y.reshape(ms, A_WIDTH), w("out_a", li))
        else:
            (xp,), converted = _gmlp(xp, i, gm, w("uv", li), lng, lnb, ws_p, bs_p, w("out_b", li),
                                     tm=TILE, sample=False, casts=casts)
            done(keys, converted)
            (xs, v), _ = _gmlp(xs, i, gm, w("uv", li), lng, lnb, ws_s, bs_s, w("out_b", li),
                               tm=ms, sample=True)
            v_rows.append(v)
            keys, casts = jobs() if last else jobs(("qkv", li + 1), ("out_a", li + 1))
            xp, converted = _ffn(xp, i, g2, w("ffn2_in", i), w("ffn2_out", i), gf, tm=TILE,
                                 final=last, casts=casts)
            done(keys, converted)
        xs, _ = _ffn(xs, i, g2, w("ffn2_in", i), w("ffn2_out", i), gf, tm=ms, final=last)

    def prompt_window(grp):
        return jnp.transpose(windows[grp], (0, 1, 5, 2, 3, 4))

    def sample_rows(grp):
        return jnp.stack([kv.reshape(DEC_BATCH, DEC_SEQ, N_GROUPS_A, 2, N_SLOTS, HEAD_DIM)[:, :, grp]
                          for kv in kv_s])

    return (xp.reshape(BATCH, SEQ, D_MODEL), xs.reshape(DEC_BATCH, DEC_SEQ, D_MODEL),
            prompt_window(0), prompt_window(1), prompt_window(2),
            sample_rows(0), sample_rows(1), sample_rows(2),
            jnp.stack(v_rows).reshape(len(v_rows), DEC_BATCH, DEC_SEQ, D_V))
```

```python
import functools

import jax
import jax.numpy as jnp
from jax import lax
from jax.experimental import pallas as pl
from jax.experimental.pallas import tpu as pltpu

F32 = jnp.float32
BF16 = jnp.bfloat16

D_MODEL = 1024
BATCH = 4
SEQ = 4096
DEPTH = 4
DEC_BATCH = 32
DEC_SEQ = 4
HEAD_DIM = 64
N_SLOTS = 8
DIL_WINDOWS = (128, 512, 2048)
DIL_RATES = (1, 4, 16)
N_GROUPS_A = 3
A_WIDTH = N_SLOTS * HEAD_DIM
QKV_WIDTH = 3 * N_GROUPS_A * A_WIDTH
KV_WIDTH = 2 * N_GROUPS_A * A_WIDTH
BAND = 128
CHUNK = 128
D_V = 3072
N_GROUPS_B = 8
GROUP_B = D_V // N_GROUPS_B
D_FF = 2816
RMS_EPS = 1e-6
LN_EPS = 1e-5
NEG = -1e30

VMEM_LIMIT_BYTES = 56 * 1024 * 1024
LANES = 128
BF16_ROWS = 16
TILE = 512
TILES_PER_SEQ = SEQ // TILE
MM_COLS = 256
FF_COLS = MM_COLS
UV_COLS = 768
Q_ROWS = 256
NEW_PAD = 16


def _params(n_axes):
    return pltpu.CompilerParams(dimension_semantics=("arbitrary",) * n_axes,
                                vmem_limit_bytes=VMEM_LIMIT_BYTES)


def _resident(shape, index_map):
    return pl.BlockSpec(shape, index_map, pipeline_mode=pl.Buffered(1))


def _rms(x, g):
    return x * lax.rsqrt(jnp.mean(x * x, axis=-1, keepdims=True) + RMS_EPS) * g


def _dot(a, b):
    return jnp.dot(a, b, preferred_element_type=F32)


def _dot_nt(a, b):
    return lax.dot_general(a, b, (((1,), (1,)), ((), ())), preferred_element_type=F32)


def _cast_plan(src, layer, n_steps):
    rows, cols = src.shape[1:]
    rb = rows // n_steps
    if rows % n_steps or rb % BF16_ROWS:
        rb = LANES
    nb = rows // rb
    assert rows % rb == 0 and nb <= n_steps
    in_spec = pl.BlockSpec((None, rb, cols), lambda i: (layer, jnp.minimum(i, nb - 1), 0))
    out_spec = pl.BlockSpec((rb, cols), lambda i: (jnp.minimum(i, nb - 1), 0))
    return in_spec, out_spec, jax.ShapeDtypeStruct((rows, cols), BF16)


def _cast_rows(src_refs, dst_refs):
    for src, dst in zip(src_refs, dst_refs, strict=True):
        dst[...] = src[...].astype(BF16)


def _merge(outs, lses):
    mx = jnp.maximum(jnp.maximum(lses[0], lses[1]), lses[2])
    e = [jnp.exp(l - mx) for l in lses]
    den = e[0] + e[1] + e[2]
    return (e[0] / den) * outs[0] + (e[1] / den) * outs[1] + (e[2] / den) * outs[2]


def _position_order(slab_ref, src_ref, d):
    n = TILE // d
    for r in range(d):
        for cc in range(A_WIDTH // LANES):
            slab_ref[cc, pl.ds(r, n, stride=d), :] = src_ref[r * n:(r + 1) * n,
                                                             cc * LANES:(cc + 1) * LANES]
    return [slab_ref[cc] for cc in range(A_WIDTH // LANES)]


def _merged_attention(o0, o1, o2, l0, l1, l2, slab_ref):
    cols = [slice(cc * LANES, (cc + 1) * LANES) for cc in range(A_WIDTH // LANES)]
    outs = [[o0[:, c] for c in cols],
            _position_order(slab_ref.at[0], o1, DIL_RATES[1]),
            _position_order(slab_ref.at[1], o2, DIL_RATES[2])]
    lses = [[l0[:, c] for c in cols],
            _position_order(slab_ref.at[2], l1, DIL_RATES[1]),
            _position_order(slab_ref.at[3], l2, DIL_RATES[2])]
    return jnp.concatenate(
        [_merge([o[cc] for o in outs], [l[cc] for l in lses]) for cc in range(len(cols))], axis=1)


def _ffn_kernel(*refs, final, mixer, n_cast, side):
    x_ref, g_ref, win_ref, wout_ref, gf_ref = refs[:5]
    refs = refs[5:]
    if mixer:
        *attn_refs, wmix_ref = refs[:7]
        refs = refs[7:]
    cast_src, refs = refs[:n_cast], refs[n_cast:]
    if side:
        xs_ref, refs = refs[0], refs[1:]
    o_ref, cast_dst, refs = refs[0], refs[1:n_cast + 1], refs[n_cast + 1:]
    if side:
        os_ref, refs = refs[0], refs[1:]
    a_ref = refs[0]

    def half_step(x):
        rows = x.shape[0]
        h = _rms(x, g_ref[...]).astype(BF16)
        for c in range(D_FF // FF_COLS):
            lo = c * FF_COLS
            gate = _dot(h, win_ref[:, lo:lo + FF_COLS])
            up = _dot(h, win_ref[:, D_FF + lo:D_FF + lo + FF_COLS])
            a_ref[:rows, lo:lo + FF_COLS] = (gate * jax.nn.sigmoid(gate) * up).astype(BF16)
        y = x + 0.5 * _dot(a_ref[:rows], wout_ref[...])
        return _rms(y, gf_ref[...]) if final else y

    _cast_rows(cast_src, cast_dst)
    x = x_ref[...]
    if mixer:
        x = x + _dot(_merged_attention(*attn_refs, refs[1]).astype(BF16), wmix_ref[...])
    o_ref[...] = half_step(x)
    if side:
        @pl.when(pl.program_id(0) == pl.num_programs(0) - 1)
        def _():
            os_ref[...] = half_step(xs_ref[...])


def _ffn(x, layer, g, w_in, w_out, g_final, *, tm, final=False, mixer=None, casts=(), side=None):
    m = x.shape[0]
    steps = m // tm
    plans = [_cast_plan(src, lyr, steps) for src, lyr in casts]
    in_specs = [
        pl.BlockSpec((tm, D_MODEL), lambda i: (i, 0)),
        pl.BlockSpec((None, 1, D_MODEL), lambda i: (layer, 0, 0)),
        _resident((D_MODEL, 2 * D_FF), lambda i: (0, 0)),
        _resident((D_FF, D_MODEL), lambda i: (0, 0)),
        pl.BlockSpec((1, D_MODEL), lambda i: (0, 0)),
    ]
    args = [x, g, w_in, w_out, g_final]
    scratch = [pltpu.VMEM((tm, D_FF), BF16)]
    if mixer is not None:
        outs, lses, w_mix = mixer
        in_specs += [pl.BlockSpec((tm, A_WIDTH), lambda i: (i, 0))] * 6
        in_specs.append(_resident((A_WIDTH, D_MODEL), lambda i: (0, 0)))
        args += [*outs, *lses, w_mix]
        scratch.append(pltpu.VMEM((4, A_WIDTH // LANES, TILE, LANES), F32))
    side_args, side_specs, side_shapes = [], [], []
    if side is not None:
        side_args = [side]
        side_specs = [pl.BlockSpec(side.shape, lambda i: (0, 0))]
        side_shapes = [jax.ShapeDtypeStruct(side.shape, F32)]
    res = pl.pallas_call(
        functools.partial(_ffn_kernel, final=final, mixer=mixer is not None, n_cast=len(plans),
                          side=side is not None),
        grid=(steps,),
        in_specs=in_specs + [p[0] for p in plans] + side_specs,
        out_specs=([pl.BlockSpec((tm, D_MODEL), lambda i: (i, 0))] + [p[1] for p in plans]
                   + side_specs),
        out_shape=[jax.ShapeDtypeStruct((m, D_MODEL), F32)] + [p[2] for p in plans] + side_shapes,
        scratch_shapes=scratch,
        compiler_params=_params(1),
        name="ffn",
    )(*args, *[src for src, _ in casts], *side_args)
    n = 1 + len(plans)
    return res[0], res[1:n], (res[n] if side is not None else None)


def _qkv_sample_kernel(x_ref, g_ref, w_ref, qkv_ref, kv_ref):
    h = _rms(x_ref[...], g_ref[...]).astype(BF16)
    for c in range(QKV_WIDTH // A_WIDTH):
        lo = c * A_WIDTH
        part, grp = divmod(c, N_GROUPS_A)
        y = _dot(h, w_ref[:, lo:lo + A_WIDTH])
        if part == 0:
            qkv_ref[:, lo:lo + A_WIDTH] = (y * (HEAD_DIM ** -0.5)).astype(BF16)
        else:
            qkv_ref[:, lo:lo + A_WIDTH] = y.astype(BF16)
            dst = (2 * grp + part - 1) * A_WIDTH
            kv_ref[:, dst:dst + A_WIDTH] = y


def _qkv_sample(x, layer, g, w):
    m = x.shape[0]
    return pl.pallas_call(
        _qkv_sample_kernel,
        grid=(1,),
        in_specs=[
            pl.BlockSpec((m, D_MODEL), lambda i: (0, 0)),
            pl.BlockSpec((None, 1, D_MODEL), lambda i: (layer, 0, 0)),
            pl.BlockSpec((D_MODEL, QKV_WIDTH), lambda i: (0, 0)),
        ],
        out_specs=[
            pl.BlockSpec((m, QKV_WIDTH), lambda i: (0, 0)),
            pl.BlockSpec((m, KV_WIDTH), lambda i: (0, 0)),
        ],
        out_shape=[jax.ShapeDtypeStruct((m, QKV_WIDTH), BF16),
                   jax.ShapeDtypeStruct((m, KV_WIDTH), F32)],
        compiler_params=_params(1),
        name="qkv_sample",
    )(x, g, w)


def _regroup_rows(slab_ref, y, dst_ref, col0, d):
    n = TILE // d
    for cc in range(A_WIDTH // LANES):
        slab_ref[cc] = y[:, cc * LANES:(cc + 1) * LANES]
    for r in range(d):
        for cc in range(A_WIDTH // LANES):
            dst_ref[r * n:(r + 1) * n, col0 + cc * LANES:col0 + (cc + 1) * LANES] = (
                slab_ref[cc, pl.ds(r, n, stride=d), :].astype(BF16))


def _qkv_prompt_kernel(*refs, n_cast, n_prev):
    x_ref, g_ref, w_ref = refs[:3]
    cast_src = refs[3 + n_prev:3 + n_prev + n_cast]
    outs = refs[3 + n_prev + n_cast:]
    dst, win = outs[:N_GROUPS_A], outs[N_GROUPS_A:2 * N_GROUPS_A]
    cast_dst = outs[2 * N_GROUPS_A:2 * N_GROUPS_A + n_cast]
    slab_ref = outs[2 * N_GROUPS_A + n_cast]
    _cast_rows(cast_src, cast_dst)
    h = _rms(x_ref[...], g_ref[...]).astype(BF16)
    for c in range(QKV_WIDTH // A_WIDTH):
        part, grp = divmod(c, N_GROUPS_A)
        y = _dot(h, w_ref[:, c * A_WIDTH:(c + 1) * A_WIDTH])
        if part == 0:
            y = y * (HEAD_DIM ** -0.5)
        col0 = part * A_WIDTH
        if grp == 0:
            dst[0][:, col0:col0 + A_WIDTH] = y.astype(BF16)
        else:
            _regroup_rows(slab_ref.at[(grp - 1) * 3 + part], y, dst[grp], col0, DIL_RATES[grp])
        if part > 0:
            keep = min(DIL_WINDOWS[grp], TILE)
            win[grp][part - 1] = y[TILE - keep:].T.reshape(N_SLOTS, HEAD_DIM, keep)


def _qkv_prompt(x, layer, g, w, windows=None, casts=()):
    m = x.shape[0]
    li = layer // 2
    steps = m // TILE
    plans = [_cast_plan(src, lyr, steps) for src, lyr in casts]
    row_spec = pl.BlockSpec((TILE, 3 * A_WIDTH), lambda i: (i, 0))
    win_specs, win_shapes = [], []
    for grp in range(N_GROUPS_A):
        keep = min(DIL_WINDOWS[grp], TILE)
        first_tile = TILES_PER_SEQ - max(DIL_WINDOWS[grp] // TILE, 1)
        win_specs.append(pl.BlockSpec(
            (None, None, 2, N_SLOTS, HEAD_DIM, keep),
            lambda i, ft=first_tile: (li, i // TILES_PER_SEQ, 0, 0, 0,
                                      jnp.maximum(i % TILES_PER_SEQ - ft, 0))))
        win_shapes.append(jax.ShapeDtypeStruct(
            (DEPTH // 2, BATCH, 2, N_SLOTS, HEAD_DIM, DIL_WINDOWS[grp]), F32))
    prev = list(windows) if windows is not None else []
    n_fixed = 3
    res = pl.pallas_call(
        functools.partial(_qkv_prompt_kernel, n_cast=len(plans), n_prev=len(prev)),
        grid=(steps,),
        in_specs=[
            pl.BlockSpec((TILE, D_MODEL), lambda i: (i, 0)),
            pl.BlockSpec((None, 1, D_MODEL), lambda i: (layer, 0, 0)),
            _resident((D_MODEL, QKV_WIDTH), lambda i: (0, 0)),
        ] + [pl.BlockSpec(memory_space=pl.ANY)] * len(prev) + [p[0] for p in plans],
        out_specs=[row_spec] * N_GROUPS_A + win_specs + [p[1] for p in plans],
        out_shape=([jax.ShapeDtypeStruct((m, 3 * A_WIDTH), BF16)] * N_GROUPS_A + win_shapes
                   + [p[2] for p in plans]),
        input_output_aliases={n_fixed + k: N_GROUPS_A + k for k in range(len(prev))},
        scratch_shapes=[pltpu.VMEM((6, A_WIDTH // LANES, TILE, LANES), F32)],
        compiler_params=_params(1),
        name="qkv_prompt",
    )(x, g, w, *prev, *[src for src, _ in casts])
    return res[:N_GROUPS_A], res[N_GROUPS_A:2 * N_GROUPS_A], res[2 * N_GROUPS_A:]


def _band_kernel(q_ref, kp_ref, ko_ref, vp_ref, vo_ref, o_ref, l_ref, s_ref, p_ref, m_ref):
    n = q_ref.shape[1]
    n_sub, n_pair = Q_ROWS // BAND, A_WIDTH // LANES
    first_key = jnp.where(pl.program_id(2) == 0, BAND, 0)
    qi = lax.broadcasted_iota(jnp.int32, (2 * BAND, 2 * BAND), 0) % BAND
    ki = lax.broadcasted_iota(jnp.int32, (2 * BAND, 2 * BAND), 1)
    band = (ki >= qi) & (ki <= qi + BAND)
    bias = jnp.where(band, 0.0, NEG)
    bias_first = jnp.where(band & (ki >= first_key), 0.0, NEG)
    low = lax.broadcasted_iota(jnp.int32, (1, LANES), 1) < HEAD_DIM
    q_all = q_ref[...].reshape(Q_ROWS, A_WIDTH)
    ko_all = ko_ref[...].reshape(Q_ROWS, A_WIDTH)
    vo_all = vo_ref[...].reshape(Q_ROWS, A_WIDTH)
    kp_all = kp_ref[...].reshape(BAND, A_WIDTH)
    vp_all = vp_ref[...].reshape(BAND, A_WIDTH)

    def keys(prev, own, j, cs):
        if j == 0:
            return jnp.concatenate([prev[:, cs], own[:BAND, cs]], axis=0)
        return own[:, cs]

    for pr in range(n_pair):
        cs = slice(pr * LANES, (pr + 1) * LANES)
        for j in range(n_sub):
            q = q_all[j * BAND:(j + 1) * BAND, cs]
            zero = jnp.zeros_like(q)
            q_ab = jnp.concatenate([jnp.where(low, q, zero), jnp.where(low, zero, q)], axis=0)
            s = _dot_nt(q_ab, keys(kp_all, ko_all, j, cs))
            s_ref[pr * n_sub + j] = s + (bias_first if j == 0 else bias)

    for u in range(n_pair * n_sub):
        mx = jnp.max(s_ref[u], axis=-1, keepdims=True)
        p_ref[u] = jnp.exp(s_ref[u] - mx).astype(BF16)
        m_ref[u] = jnp.where(low, mx[:BAND], mx[BAND:])

    one = jnp.ones((2 * BAND, LANES), BF16)
    for pr in range(n_pair):
        cs = slice(pr * LANES, (pr + 1) * LANES)
        for j in range(n_sub):
            u = pr * n_sub + j
            vv = keys(vp_all, vo_all, j, cs)
            oa = _dot(p_ref[u, :BAND], jnp.where(low, vv, one))
            ob = _dot(p_ref[u, BAND:], jnp.where(low, one, vv))
            den = pltpu.roll(jnp.where(low, ob, oa), HEAD_DIM, axis=1)
            lse = m_ref[u] + jnp.log(den)
            tiles = slice(j * (BAND // n), (j + 1) * (BAND // n))
            o_ref[tiles, :, cs] = (jnp.where(low, oa, ob) / den).reshape(BAND // n, n, LANES)
            l_ref[tiles, :, cs] = lse.reshape(BAND // n, n, LANES)


def _band_attention(qkv, grp):
    d = DIL_RATES[grp]
    n = min(TILE // d, BAND)
    pieces = BATCH * SEQ // (d * n)
    nb = SEQ // (d * Q_ROWS)
    units = (Q_ROWS // BAND) * (A_WIDTH // LANES)
    view = qkv.reshape(pieces, d, n, 3 * A_WIDTH)

    def own(part):
        return pl.BlockSpec((Q_ROWS // n, None, n, A_WIDTH), lambda b, r, i: (b * nb + i, r, 0, part))

    def prev(part):
        return pl.BlockSpec((BAND // n, None, n, A_WIDTH),
                            lambda b, r, i: (jnp.maximum((b * nb + i) * (Q_ROWS // BAND) - 1, 0),
                                             r, 0, part))

    out_spec = pl.BlockSpec((Q_ROWS // n, None, n, A_WIDTH), lambda b, r, i: (b * nb + i, r, 0, 0))
    out_shape = jax.ShapeDtypeStruct((pieces, d, n, A_WIDTH), F32)
    o, lse = pl.pallas_call(
        _band_kernel,
        grid=(BATCH, d, nb),
        in_specs=[own(0), prev(1), own(1), prev(2), own(2)],
        out_specs=[out_spec, out_spec],
        out_shape=[out_shape, out_shape],
        scratch_shapes=[pltpu.VMEM((units, 2 * BAND, 2 * BAND), F32),
                        pltpu.VMEM((units, 2 * BAND, 2 * BAND), BF16),
                        pltpu.VMEM((units, BAND, LANES), F32)],
        compiler_params=_params(3),
        name=f"band_attention_d{d}",
    )(view, view, view, view, view)
    return o.reshape(BATCH * SEQ, A_WIDTH), lse.reshape(BATCH * SEQ, A_WIDTH)


def _sample_attn_kernel(qkv_ref, c0_ref, c1_ref, c2_ref, y_ref):
    n_rows = DEC_SEQ * N_SLOTS
    qkv = qkv_ref[...].astype(F32)
    row_h = lax.broadcasted_iota(jnp.int32, (n_rows, A_WIDTH), 0) % N_SLOTS
    col_h = lax.broadcasted_iota(jnp.int32, (n_rows, A_WIDTH), 1) // HEAD_DIM
    own_head = row_h == col_h
    pad = jnp.zeros((NEW_PAD - DEC_SEQ, A_WIDTH), F32)

    def reach(n_keys, offset, d):
        t = lax.broadcasted_iota(jnp.int32, (n_rows, n_keys), 0) // N_SLOTS
        back = t - lax.broadcasted_iota(jnp.int32, (n_rows, n_keys), 1) - offset
        return (back >= 0) & (back <= BAND * d) & ((back & (d - 1)) == 0)

    outs, lses = [], []
    for grp, c_ref in enumerate((c0_ref, c1_ref, c2_ref)):
        d, window = DIL_RATES[grp], DIL_WINDOWS[grp]
        q = qkv[:, grp * A_WIDTH:(grp + 1) * A_WIDTH]
        k_new = qkv[:, (N_GROUPS_A + grp) * A_WIDTH:(N_GROUPS_A + grp + 1) * A_WIDTH]
        v_new = qkv[:, (2 * N_GROUPS_A + grp) * A_WIDTH:(2 * N_GROUPS_A + grp + 1) * A_WIDTH]
        q_rep = jnp.concatenate(
            [jnp.broadcast_to(q[t:t + 1], (N_SLOTS, A_WIDTH)) for t in range(DEC_SEQ)], axis=0)
        q_bd = jnp.where(own_head, q_rep, 0.0).astype(BF16)
        k_new = jnp.concatenate([k_new, pad], axis=0).astype(BF16)
        v_new = jnp.concatenate([v_new, pad], axis=0).astype(BF16)
        s_c = jnp.where(reach(window, -window, d), _dot(q_bd, c_ref[0].astype(BF16)), NEG)
        s_n = jnp.where(reach(NEW_PAD, 0, d), _dot_nt(q_bd, k_new), NEG)
        mx = jnp.maximum(jnp.max(s_c, axis=-1, keepdims=True), jnp.max(s_n, axis=-1, keepdims=True))
        p_c = jnp.exp(s_c - mx)
        p_n = jnp.exp(s_n - mx)
        den = jnp.sum(p_c, axis=-1, keepdims=True) + jnp.sum(p_n, axis=-1, keepdims=True)
        o = (_dot_nt(p_c.astype(BF16), c_ref[1].astype(BF16)) + _dot(p_n.astype(BF16), v_new)) / den
        lse = jnp.broadcast_to(mx + jnp.log(den), (n_rows, A_WIDTH))
        o = jnp.where(own_head, o, 0.0)
        lse = jnp.where(own_head, lse, 0.0)
        outs.append(jnp.concatenate(
            [jnp.sum(o[t * N_SLOTS:(t + 1) * N_SLOTS], axis=0, keepdims=True) for t in range(DEC_SEQ)],
            axis=0))
        lses.append(jnp.concatenate(
            [jnp.sum(lse[t * N_SLOTS:(t + 1) * N_SLOTS], axis=0, keepdims=True) for t in range(DEC_SEQ)],
            axis=0))
    y_ref[...] = _merge(outs, lses)


def _sample_attention(qkv, caches, li):
    views = [jnp.transpose(c, (0, 1, 3, 4, 5, 2)).reshape(-1, DEC_BATCH, 2, A_WIDTH, c.shape[2])
             for c in caches]
    return pl.pallas_call(
        _sample_attn_kernel,
        grid=(DEC_BATCH,),
        in_specs=[pl.BlockSpec((None, DEC_SEQ, QKV_WIDTH), lambda b: (b, 0, 0))] + [
            pl.BlockSpec((None, None, 2, A_WIDTH, w), lambda b: (li, b, 0, 0, 0)) for w in DIL_WINDOWS],
        out_specs=pl.BlockSpec((None, DEC_SEQ, A_WIDTH), lambda b: (b, 0, 0)),
        out_shape=jax.ShapeDtypeStruct((DEC_BATCH, DEC_SEQ, A_WIDTH), F32),
        compiler_params=_params(1),
        name="sample_attention",
    )(qkv.reshape(DEC_BATCH, DEC_SEQ, QKV_WIDTH), *views)


def _proj_kernel(x_ref, y_ref, w_ref, out_ref):
    out_ref[...] = x_ref[...] + _dot(y_ref[...].astype(BF16), w_ref[...])


def _proj(x, y, w):
    m = x.shape[0]
    return pl.pallas_call(
        _proj_kernel,
        grid=(1,),
        in_specs=[pl.BlockSpec((m, D_MODEL), lambda i: (0, 0)),
                  pl.BlockSpec((m, A_WIDTH), lambda i: (0, 0)),
                  pl.BlockSpec((A_WIDTH, D_MODEL), lambda i: (0, 0))],
        out_specs=pl.BlockSpec((m, D_MODEL), lambda i: (0, 0)),
        out_shape=jax.ShapeDtypeStruct((m, D_MODEL), F32),
        compiler_params=_params(1),
        name="proj",
    )(x, y, w)


def _gelu(x):
    a0 = -2.0 * (2.0 / jnp.pi) ** 0.5
    return x / (1.0 + jnp.exp(x * (a0 + (a0 * 0.044715) * (x * x))))


def _gmlp_kernel(*refs, tm, sample, n_cast):
    x_ref, g_ref, wuv_ref, lng_ref, lnb_ref, ws_ref, bs_ref, wo_ref = refs[:8]
    cast_src = refs[8:8 + n_cast]
    refs = refs[8 + n_cast:]
    out_ref = refs[0]
    if sample:
        v_ref = refs[1]
    refs = refs[2:] if sample else refs[1:]
    cast_dst = refs[:n_cast]
    zv_ref, vn_ref, um_ref = refs[n_cast:]
    _cast_rows(cast_src, cast_dst)
    x = x_ref[...]
    h = _rms(x, g_ref[...]).astype(BF16)
    n_uv = D_V // UV_COLS

    tot = jnp.zeros((tm, 1), F32)
    for c in range(n_uv):
        lo = c * UV_COLS
        z = _gelu(_dot(h, wuv_ref[:, D_V + lo:D_V + lo + UV_COLS]))
        zv_ref[:, lo:lo + UV_COLS] = z
        tot = tot + jnp.sum(z, axis=-1, keepdims=True)
    mu = tot / D_V
    sq = jnp.zeros((tm, 1), F32)
    for c in range(n_uv):
        zc = zv_ref[:, c * UV_COLS:(c + 1) * UV_COLS] - mu
        sq = sq + jnp.sum(zc * zc, axis=-1, keepdims=True)
    rstd = lax.rsqrt(sq / D_V + LN_EPS)
    for c in range(n_uv):
        cols = slice(c * UV_COLS, (c + 1) * UV_COLS)
        vn = (zv_ref[:, cols] - mu) * rstd * lng_ref[:, cols] + lnb_ref[:, cols]
        vn_ref[:, cols] = vn.astype(BF16)
        if sample:
            v_ref[:, cols] = vn

    ri = lax.broadcasted_iota(jnp.int32, (CHUNK, CHUNK), 0)
    ci = lax.broadcasted_iota(jnp.int32, (CHUNK, CHUNK), 1)
    causal = ci <= ri
    if sample:
        causal = causal & ((ri // DEC_SEQ) == (ci // DEC_SEQ))
    groups_per_mm = UV_COLS // GROUP_B
    for c in range(n_uv):
        u = _gelu(_dot(h, wuv_ref[:, c * UV_COLS:(c + 1) * UV_COLS]))
        for gl in range(groups_per_mm):
            grp = c * groups_per_mm + gl
            w = jnp.where(causal, ws_ref[grp], 0.0).astype(BF16)
            bias = bs_ref[:, grp:grp + 1]
            cols = slice(grp * GROUP_B, (grp + 1) * GROUP_B)
            for n in range(tm // CHUNK):
                rows = slice(n * CHUNK, (n + 1) * CHUNK)
                mixed = _dot(w, vn_ref[rows, cols]) + bias
                um_ref[rows, cols] = (u[rows, gl * GROUP_B:(gl + 1) * GROUP_B] * mixed).astype(BF16)
    out_ref[...] = x + _dot(um_ref[...], wo_ref[...])


def _gmlp(x, layer, g, w_uv, ln_g, ln_b, w_s, b_s, w_out, *, tm, sample, casts=()):
    m = x.shape[0]
    li = layer // 2
    steps = m // tm
    plans = [_cast_plan(src, lyr, steps) for src, lyr in casts]
    out_specs = [pl.BlockSpec((tm, D_MODEL), lambda i: (i, 0))]
    out_shape = [jax.ShapeDtypeStruct((m, D_MODEL), F32)]
    if sample:
        out_specs.append(pl.BlockSpec((tm, D_V), lambda i: (i, 0)))
        out_shape.append(jax.ShapeDtypeStruct((m, D_V), F32))
    n_main = len(out_specs)
    res = pl.pallas_call(
        functools.partial(_gmlp_kernel, tm=tm, sample=sample, n_cast=len(plans)),
        grid=(steps,),
        in_specs=[
            pl.BlockSpec((tm, D_MODEL), lambda i: (i, 0)),
            pl.BlockSpec((None, 1, D_MODEL), lambda i: (layer, 0, 0)),
            _resident((D_MODEL, 2 * D_V), lambda i: (0, 0)),
            pl.BlockSpec((None, 1, D_V), lambda i: (li, 0, 0)),
            pl.BlockSpec((None, 1, D_V), lambda i: (li, 0, 0)),
            pl.BlockSpec((None, N_GROUPS_B, CHUNK, CHUNK), lambda i: (li, 0, 0, 0)),
            pl.BlockSpec((None, CHUNK, N_GROUPS_B), lambda i: (li, 0, 0)),
            _resident((D_V, D_MODEL), lambda i: (0, 0)),
        ] + [p[0] for p in plans],
        out_specs=out_specs + [p[1] for p in plans],
        out_shape=out_shape + [p[2] for p in plans],
        scratch_shapes=[pltpu.VMEM((tm, D_V), F32), pltpu.VMEM((tm, D_V), BF16),
                        pltpu.VMEM((tm, D_V), BF16)],
        compiler_params=_params(1),
        name="gmlp",
    )(x, g, w_uv, ln_g, ln_b, w_s, b_s, w_out, *[src for src, _ in casts])
    return res[:n_main], res[n_main:]


def kernel(x_prompt, x_sample, cache_kv_w128, cache_kv_w512, cache_kv_w2048, norm_ffn1, w_ffn1_in,
           w_ffn1_out, norm_mix, norm_ffn2, w_ffn2_in, w_ffn2_out, w_qkv_a, w_out_a, w_uv_b,
           ln_v_gain, ln_v_bias, w_spatial, b_spatial, w_out_b, norm_final):
    caches = (cache_kv_w128, cache_kv_w512, cache_kv_w2048)
    mp, ms = BATCH * SEQ, DEC_BATCH * DEC_SEQ
    xp = x_prompt.reshape(mp, D_MODEL)
    xs = x_sample.reshape(ms, D_MODEL)

    g1 = norm_ffn1.reshape(DEPTH, 1, D_MODEL)
    gm = norm_mix.reshape(DEPTH, 1, D_MODEL)
    g2 = norm_ffn2.reshape(DEPTH, 1, D_MODEL)
    gf = norm_final.reshape(1, D_MODEL)
    lng = ln_v_gain.reshape(-1, 1, D_V)
    lnb = ln_v_bias.reshape(-1, 1, D_V)
    reps = CHUNK // DEC_SEQ
    ws_p = w_spatial
    bs_p = jnp.swapaxes(b_spatial, 1, 2)
    ws_s = jnp.tile(w_spatial[:, :, :DEC_SEQ, :DEC_SEQ], (1, 1, reps, reps))
    bs_s = jnp.swapaxes(jnp.tile(b_spatial[:, :, :DEC_SEQ], (1, 1, reps)), 1, 2)

    f32_weights = {"ffn1_in": w_ffn1_in, "ffn1_out": w_ffn1_out, "ffn2_in": w_ffn2_in,
                   "ffn2_out": w_ffn2_out, "qkv": w_qkv_a, "out_a": w_out_a, "uv": w_uv_b,
                   "out_b": w_out_b}
    bf16_weights = {("ffn1_in", 0): w_ffn1_in[0].astype(BF16),
                    ("ffn1_out", 0): w_ffn1_out[0].astype(BF16)}

    def jobs(*keys):
        return keys, tuple((f32_weights[name], idx) for name, idx in keys)

    def done(keys, converted):
        bf16_weights.update(zip(keys, converted, strict=True))

    def w(name, idx):
        return bf16_weights[(name, idx)]

    windows, kv_s, v_rows = None, [], []
    for i in range(DEPTH):
        li = i // 2
        attention = i % 2 == 0
        last = i == DEPTH - 1

        if not attention:
            keys, casts = jobs(("uv", li), ("out_b", li), ("ffn2_in", i), ("ffn2_out", i))
        elif i == 0:
            keys, casts = jobs(("qkv", li), ("out_a", li), ("ffn2_in", i), ("ffn2_out", i))
        else:
            keys, casts = jobs(("ffn2_in", i), ("ffn2_out", i))
        xp, converted, xs = _ffn(xp, i, g1, w("ffn1_in", i), w("ffn1_out", i), gf, tm=TILE,
                                 casts=casts, side=xs)
        done(keys, converted)

        keys, casts = jobs() if last else jobs(("ffn1_in", i + 1), ("ffn1_out", i + 1))
        if attention:
            qkv_g, windows, converted = _qkv_prompt(xp, i, gm, w("qkv", li), windows, casts)
            done(keys, converted)
            parts = [_band_attention(qkv_g[grp], grp) for grp in range(N_GROUPS_A)]
            mixer = ([p[0] for p in parts], [p[1] for p in parts], w("out_a", li))
            qkv, kv = _qkv_sample(xs, i, gm, w("qkv", li))
            kv_s.append(kv)
            y = _sample_attention(qkv, caches, li)
            xs = _proj(xs, y.reshape(ms, A_WIDTH), w("out_a", li))
            xp, _, xs = _ffn(xp, i, g2, w("ffn2_in", i), w("ffn2_out", i), gf, tm=TILE, final=last,
                             mixer=mixer, side=xs)
        else:
            (xp,), converted = _gmlp(xp, i, gm, w("uv", li), lng, lnb, ws_p, bs_p, w("out_b", li),
                                     tm=TILE, sample=False, casts=casts)
            done(keys, converted)
            (xs, v), _ = _gmlp(xs, i, gm, w("uv", li), lng, lnb, ws_s, bs_s, w("out_b", li),
                               tm=ms, sample=True)
            v_rows.append(v)
            keys, casts = jobs() if last else jobs(("qkv", li + 1), ("out_a", li + 1))
            xp, converted, xs = _ffn(xp, i, g2, w("ffn2_in", i), w("ffn2_out", i), gf, tm=TILE,
                                     final=last, casts=casts, side=xs)
            done(keys, converted)

    def prompt_window(grp):
        return jnp.transpose(windows[grp], (0, 1, 5, 2, 3, 4))

    def sample_rows(grp):
        return jnp.stack([kv.reshape(DEC_BATCH, DEC_SEQ, N_GROUPS_A, 2, N_SLOTS, HEAD_DIM)[:, :, grp]
                          for kv in kv_s])

    return (xp.reshape(BATCH, SEQ, D_MODEL), xs.reshape(DEC_BATCH, DEC_SEQ, D_MODEL),
            prompt_window(0), prompt_window(1), prompt_window(2),
            sample_rows(0), sample_rows(1), sample_rows(2),
            jnp.stack(v_rows).reshape(len(v_rows), DEC_BATCH, DEC_SEQ, D_V))
```

```python
import functools

import jax
import jax.numpy as jnp
from jax import lax
from jax.experimental import pallas as pl
from jax.experimental.pallas import tpu as pltpu

F32 = jnp.float32
BF16 = jnp.bfloat16

D_MODEL = 1024
BATCH = 4
SEQ = 4096
DEPTH = 4
DEC_BATCH = 32
DEC_SEQ = 4
HEAD_DIM = 64
N_SLOTS = 8
DIL_WINDOWS = (128, 512, 2048)
DIL_RATES = (1, 4, 16)
N_GROUPS_A = 3
A_WIDTH = N_SLOTS * HEAD_DIM
QKV_WIDTH = 3 * N_GROUPS_A * A_WIDTH
KV_WIDTH = 2 * N_GROUPS_A * A_WIDTH
BAND = 128
CHUNK = 128
D_V = 3072
N_GROUPS_B = 8
GROUP_B = D_V // N_GROUPS_B
D_FF = 2816
RMS_EPS = 1e-6
LN_EPS = 1e-5
NEG = -1e30

VMEM_LIMIT_BYTES = 56 * 1024 * 1024
LANES = 128
BF16_ROWS = 16
FREE_STRIDE = 4
TILE = 512
TILES_PER_SEQ = SEQ // TILE
MM_COLS = 256
FF_COLS = MM_COLS
UV_COLS = 768
Q_ROWS = 256
NEW_PAD = 16


def _params(n_axes):
    return pltpu.CompilerParams(dimension_semantics=("arbitrary",) * n_axes,
                                vmem_limit_bytes=VMEM_LIMIT_BYTES)


def _resident(shape, index_map):
    return pl.BlockSpec(shape, index_map, pipeline_mode=pl.Buffered(1))


def _rms(x, g):
    return x * lax.rsqrt(jnp.mean(x * x, axis=-1, keepdims=True) + RMS_EPS) * g


def _dot(a, b):
    return jnp.dot(a, b, preferred_element_type=F32)


def _dot_nt(a, b):
    return lax.dot_general(a, b, (((1,), (1,)), ((), ())), preferred_element_type=F32)


def _cast_plan(src, layer, n_steps):
    rows, cols = src.shape[1:]
    rb = rows // n_steps
    if rows % n_steps or rb % BF16_ROWS:
        rb = LANES
    nb = rows // rb
    assert rows % rb == 0 and nb <= n_steps
    in_spec = pl.BlockSpec((None, rb, cols), lambda i: (layer, jnp.minimum(i, nb - 1), 0))
    out_spec = pl.BlockSpec((rb, cols), lambda i: (jnp.minimum(i, nb - 1), 0))
    return in_spec, out_spec, jax.ShapeDtypeStruct((rows, cols), BF16)


def _cast_rows(src_refs, dst_refs):
    for src, dst in zip(src_refs, dst_refs, strict=True):
        dst[...] = src[...].astype(BF16)


def _merge(outs, lses):
    mx = jnp.maximum(jnp.maximum(lses[0], lses[1]), lses[2])
    e = [jnp.exp(l - mx) for l in lses]
    den = e[0] + e[1] + e[2]
    return (e[0] / den) * outs[0] + (e[1] / den) * outs[1] + (e[2] / den) * outs[2]


def _position_order(slab_ref, slab2_ref, src_ref, d):
    n = TILE // d
    n_cols = A_WIDTH // LANES
    two_pass = d > FREE_STRIDE
    per = d // FREE_STRIDE if two_pass else d
    m = TILE // FREE_STRIDE
    dst = slab2_ref if two_pass else slab_ref
    for k in range(d):
        r0, r1 = divmod(k, per)
        for cc in range(n_cols):
            dst[cc, pl.ds(r0 * m * two_pass + r1, n, stride=per), :] = (
                src_ref[k * n:(k + 1) * n, cc * LANES:(cc + 1) * LANES])
    if two_pass:
        for r0 in range(FREE_STRIDE):
            for cc in range(n_cols):
                slab_ref[cc, pl.ds(r0, m, stride=FREE_STRIDE), :] = slab2_ref[cc, r0 * m:(r0 + 1) * m, :]
    return [slab_ref[cc] for cc in range(n_cols)]


def _merged_attention(o0, o1, o2, l0, l1, l2, slab_ref):
    cols = [slice(cc * LANES, (cc + 1) * LANES) for cc in range(A_WIDTH // LANES)]
    outs = [[o0[:, c] for c in cols],
            _position_order(slab_ref.at[0], None, o1, DIL_RATES[1]),
            _position_order(slab_ref.at[1], slab_ref.at[4], o2, DIL_RATES[2])]
    lses = [[l0[:, c] for c in cols],
            _position_order(slab_ref.at[2], None, l1, DIL_RATES[1]),
            _position_order(slab_ref.at[3], slab_ref.at[5], l2, DIL_RATES[2])]
    return jnp.concatenate(
        [_merge([o[cc] for o in outs], [l[cc] for l in lses]) for cc in range(len(cols))], axis=1)


def _ffn_kernel(*refs, final, mixer, n_cast, side):
    x_ref, g_ref, win_ref, wout_ref, gf_ref = refs[:5]
    refs = refs[5:]
    if mixer:
        *attn_refs, wmix_ref = refs[:7]
        refs = refs[7:]
    cast_src, refs = refs[:n_cast], refs[n_cast:]
    if side:
        xs_ref, refs = refs[0], refs[1:]
    o_ref, cast_dst, refs = refs[0], refs[1:n_cast + 1], refs[n_cast + 1:]
    if side:
        os_ref, refs = refs[0], refs[1:]
    a_ref = refs[0]

    def half_step(x):
        rows = x.shape[0]
        h = _rms(x, g_ref[...]).astype(BF16)
        for c in range(D_FF // FF_COLS):
            lo = c * FF_COLS
            gate = _dot(h, win_ref[:, lo:lo + FF_COLS])
            up = _dot(h, win_ref[:, D_FF + lo:D_FF + lo + FF_COLS])
            a_ref[:rows, lo:lo + FF_COLS] = (gate * jax.nn.sigmoid(gate) * up).astype(BF16)
        y = x + 0.5 * _dot(a_ref[:rows], wout_ref[...])
        return _rms(y, gf_ref[...]) if final else y

    _cast_rows(cast_src, cast_dst)
    x = x_ref[...]
    if mixer:
        x = x + _dot(_merged_attention(*attn_refs, refs[1]).astype(BF16), wmix_ref[...])
    o_ref[...] = half_step(x)
    if side:
        @pl.when(pl.program_id(0) == pl.num_programs(0) - 1)
        def _():
            os_ref[...] = half_step(xs_ref[...])


def _ffn(x, layer, g, w_in, w_out, g_final, *, tm, final=False, mixer=None, casts=(), side=None):
    m = x.shape[0]
    steps = m // tm
    plans = [_cast_plan(src, lyr, steps) for src, lyr in casts]
    in_specs = [
        pl.BlockSpec((tm, D_MODEL), lambda i: (i, 0)),
        pl.BlockSpec((None, 1, D_MODEL), lambda i: (layer, 0, 0)),
        _resident((D_MODEL, 2 * D_FF), lambda i: (0, 0)),
        _resident((D_FF, D_MODEL), lambda i: (0, 0)),
        pl.BlockSpec((1, D_MODEL), lambda i: (0, 0)),
    ]
    args = [x, g, w_in, w_out, g_final]
    scratch = [pltpu.VMEM((tm, D_FF), BF16)]
    if mixer is not None:
        outs, lses, w_mix = mixer
        in_specs += [pl.BlockSpec((tm, A_WIDTH), lambda i: (i, 0))] * 6
        in_specs.append(_resident((A_WIDTH, D_MODEL), lambda i: (0, 0)))
        args += [*outs, *lses, w_mix]
        scratch.append(pltpu.VMEM((6, A_WIDTH // LANES, TILE, LANES), F32))
    side_args, side_specs, side_shapes = [], [], []
    if side is not None:
        side_args = [side]
        side_specs = [pl.BlockSpec(side.shape, lambda i: (0, 0))]
        side_shapes = [jax.ShapeDtypeStruct(side.shape, F32)]
    res = pl.pallas_call(
        functools.partial(_ffn_kernel, final=final, mixer=mixer is not None, n_cast=len(plans),
                          side=side is not None),
        grid=(steps,),
        in_specs=in_specs + [p[0] for p in plans] + side_specs,
        out_specs=([pl.BlockSpec((tm, D_MODEL), lambda i: (i, 0))] + [p[1] for p in plans]
                   + side_specs),
        out_shape=[jax.ShapeDtypeStruct((m, D_MODEL), F32)] + [p[2] for p in plans] + side_shapes,
        scratch_shapes=scratch,
        compiler_params=_params(1),
        name="ffn",
    )(*args, *[src for src, _ in casts], *side_args)
    n = 1 + len(plans)
    return res[0], res[1:n], (res[n] if side is not None else None)


def _qkv_sample_kernel(x_ref, g_ref, w_ref, qkv_ref, kv_ref):
    h = _rms(x_ref[...], g_ref[...]).astype(BF16)
    for c in range(QKV_WIDTH // A_WIDTH):
        lo = c * A_WIDTH
        part, grp = divmod(c, N_GROUPS_A)
        y = _dot(h, w_ref[:, lo:lo + A_WIDTH])
        if part == 0:
            qkv_ref[:, lo:lo + A_WIDTH] = (y * (HEAD_DIM ** -0.5)).astype(BF16)
        else:
            qkv_ref[:, lo:lo + A_WIDTH] = y.astype(BF16)
            dst = (2 * grp + part - 1) * A_WIDTH
            kv_ref[:, dst:dst + A_WIDTH] = y


def _qkv_sample(x, layer, g, w):
    m = x.shape[0]
    return pl.pallas_call(
        _qkv_sample_kernel,
        grid=(1,),
        in_specs=[
            pl.BlockSpec((m, D_MODEL), lambda i: (0, 0)),
            pl.BlockSpec((None, 1, D_MODEL), lambda i: (layer, 0, 0)),
            pl.BlockSpec((D_MODEL, QKV_WIDTH), lambda i: (0, 0)),
        ],
        out_specs=[
            pl.BlockSpec((m, QKV_WIDTH), lambda i: (0, 0)),
            pl.BlockSpec((m, KV_WIDTH), lambda i: (0, 0)),
        ],
        out_shape=[jax.ShapeDtypeStruct((m, QKV_WIDTH), BF16),
                   jax.ShapeDtypeStruct((m, KV_WIDTH), F32)],
        compiler_params=_params(1),
        name="qkv_sample",
    )(x, g, w)


def _regroup_rows(slab_ref, slab2_ref, y, dst_ref, col0, d):
    n = TILE // d
    cols = [slice(cc * LANES, (cc + 1) * LANES) for cc in range(A_WIDTH // LANES)]
    for cc, c in enumerate(cols):
        slab_ref[cc] = y[:, c]
    if d > FREE_STRIDE:
        m = TILE // FREE_STRIDE
        for r0 in range(FREE_STRIDE):
            for cc in range(len(cols)):
                slab2_ref[cc, r0 * m:(r0 + 1) * m, :] = slab_ref[cc, pl.ds(r0, m, stride=FREE_STRIDE), :]
        src, per, step = slab2_ref, d // FREE_STRIDE, m
    else:
        src, per, step = slab_ref, d, 0
    for k in range(d):
        r0, r1 = divmod(k, per)
        for cc, c in enumerate(cols):
            dst_ref[k * n:(k + 1) * n, col0 + c.start:col0 + c.stop] = (
                src[cc, pl.ds(r0 * step + r1, n, stride=per), :].astype(BF16))


def _qkv_prompt_kernel(*refs, n_cast, n_prev):
    x_ref, g_ref, w_ref = refs[:3]
    cast_src = refs[3 + n_prev:3 + n_prev + n_cast]
    outs = refs[3 + n_prev + n_cast:]
    dst, win = outs[:N_GROUPS_A], outs[N_GROUPS_A:2 * N_GROUPS_A]
    cast_dst = outs[2 * N_GROUPS_A:2 * N_GROUPS_A + n_cast]
    slab_ref = outs[2 * N_GROUPS_A + n_cast]
    _cast_rows(cast_src, cast_dst)
    h = _rms(x_ref[...], g_ref[...]).astype(BF16)
    for c in range(QKV_WIDTH // A_WIDTH):
        part, grp = divmod(c, N_GROUPS_A)
        y = _dot(h, w_ref[:, c * A_WIDTH:(c + 1) * A_WIDTH])
        if part == 0:
            y = y * (HEAD_DIM ** -0.5)
        col0 = part * A_WIDTH
        if grp == 0:
            dst[0][:, col0:col0 + A_WIDTH] = y.astype(BF16)
        else:
            _regroup_rows(slab_ref.at[(grp - 1) * 3 + part], slab_ref.at[6 + part], y, dst[grp],
                          col0, DIL_RATES[grp])
        if part > 0:
            keep = min(DIL_WINDOWS[grp], TILE)
            win[grp][part - 1] = y[TILE - keep:].T.reshape(N_SLOTS, HEAD_DIM, keep)


def _qkv_prompt(x, layer, g, w, windows=None, casts=()):
    m = x.shape[0]
    li = layer // 2
    steps = m // TILE
    plans = [_cast_plan(src, lyr, steps) for src, lyr in casts]
    row_spec = pl.BlockSpec((TILE, 3 * A_WIDTH), lambda i: (i, 0))
    win_specs, win_shapes = [], []
    for grp in range(N_GROUPS_A):
        keep = min(DIL_WINDOWS[grp], TILE)
        first_tile = TILES_PER_SEQ - max(DIL_WINDOWS[grp] // TILE, 1)
        win_specs.append(pl.BlockSpec(
            (None, None, 2, N_SLOTS, HEAD_DIM, keep),
            lambda i, ft=first_tile: (li, i // TILES_PER_SEQ, 0, 0, 0,
                                      jnp.maximum(i % TILES_PER_SEQ - ft, 0))))
        win_shapes.append(jax.ShapeDtypeStruct(
            (DEPTH // 2, BATCH, 2, N_SLOTS, HEAD_DIM, DIL_WINDOWS[grp]), F32))
    prev = list(windows) if windows is not None else []
    n_fixed = 3
    res = pl.pallas_call(
        functools.partial(_qkv_prompt_kernel, n_cast=len(plans), n_prev=len(prev)),
        grid=(steps,),
        in_specs=[
            pl.BlockSpec((TILE, D_MODEL), lambda i: (i, 0)),
            pl.BlockSpec((None, 1, D_MODEL), lambda i: (layer, 0, 0)),
            _resident((D_MODEL, QKV_WIDTH), lambda i: (0, 0)),
        ] + [pl.BlockSpec(memory_space=pl.ANY)] * len(prev) + [p[0] for p in plans],
        out_specs=[row_spec] * N_GROUPS_A + win_specs + [p[1] for p in plans],
        out_shape=([jax.ShapeDtypeStruct((m, 3 * A_WIDTH), BF16)] * N_GROUPS_A + win_shapes
                   + [p[2] for p in plans]),
        input_output_aliases={n_fixed + k: N_GROUPS_A + k for k in range(len(prev))},
        scratch_shapes=[pltpu.VMEM((9, A_WIDTH // LANES, TILE, LANES), F32)],
        compiler_params=_params(1),
        name="qkv_prompt",
    )(x, g, w, *prev, *[src for src, _ in casts])
    return res[:N_GROUPS_A], res[N_GROUPS_A:2 * N_GROUPS_A], res[2 * N_GROUPS_A:]


def _band_kernel(mask_ref, q_ref, kp_ref, ko_ref, vp_ref, vo_ref, o_ref, l_ref, s_ref, p_ref, m_ref):
    n = q_ref.shape[1]
    n_sub, n_pair = Q_ROWS // BAND, A_WIDTH // LANES
    bias = mask_ref[0]
    bias_first = mask_ref[(pl.program_id(2) == 0).astype(jnp.int32)]
    low = lax.broadcasted_iota(jnp.int32, (1, LANES), 1) < HEAD_DIM
    q_all = q_ref[...].reshape(Q_ROWS, A_WIDTH)
    ko_all = ko_ref[...].reshape(Q_ROWS, A_WIDTH)
    vo_all = vo_ref[...].reshape(Q_ROWS, A_WIDTH)
    kp_all = kp_ref[...].reshape(BAND, A_WIDTH)
    vp_all = vp_ref[...].reshape(BAND, A_WIDTH)

    def keys(prev, own, j, cs):
        if j == 0:
            return jnp.concatenate([prev[:, cs], own[:BAND, cs]], axis=0)
        return own[:, cs]

    for pr in range(n_pair):
        cs = slice(pr * LANES, (pr + 1) * LANES)
        for j in range(n_sub):
            q = q_all[j * BAND:(j + 1) * BAND, cs]
            zero = jnp.zeros_like(q)
            q_ab = jnp.concatenate([jnp.where(low, q, zero), jnp.where(low, zero, q)], axis=0)
            s = _dot_nt(q_ab, keys(kp_all, ko_all, j, cs))
            s_ref[pr * n_sub + j] = s + (bias_first if j == 0 else bias)

    for u in range(n_pair * n_sub):
        mx = jnp.max(s_ref[u], axis=-1, keepdims=True)
        p_ref[u] = jnp.exp(s_ref[u] - mx).astype(BF16)
        m_ref[u] = jnp.where(low, mx[:BAND], mx[BAND:])

    one = jnp.ones((2 * BAND, LANES), BF16)
    for pr in range(n_pair):
        cs = slice(pr * LANES, (pr + 1) * LANES)
        for j in range(n_sub):
            u = pr * n_sub + j
            vv = keys(vp_all, vo_all, j, cs)
            oa = _dot(p_ref[u, :BAND], jnp.where(low, vv, one))
            ob = _dot(p_ref[u, BAND:], jnp.where(low, one, vv))
            den = pltpu.roll(jnp.where(low, ob, oa), HEAD_DIM, axis=1)
            lse = m_ref[u] + jnp.log(den)
            tiles = slice(j * (BAND // n), (j + 1) * (BAND // n))
            o_ref[tiles, :, cs] = (jnp.where(low, oa, ob) / den).reshape(BAND // n, n, LANES)
            l_ref[tiles, :, cs] = lse.reshape(BAND // n, n, LANES)


def _band_attention(qkv, grp):
    d = DIL_RATES[grp]
    n = min(TILE // d, BAND)
    pieces = BATCH * SEQ // (d * n)
    nb = SEQ // (d * Q_ROWS)
    units = (Q_ROWS // BAND) * (A_WIDTH // LANES)
    view = qkv.reshape(pieces, d, n, 3 * A_WIDTH)

    def own(part):
        return pl.BlockSpec((Q_ROWS // n, None, n, A_WIDTH), lambda b, r, i: (b * nb + i, r, 0, part))

    def prev(part):
        return pl.BlockSpec((BAND // n, None, n, A_WIDTH),
                            lambda b, r, i: (jnp.maximum((b * nb + i) * (Q_ROWS // BAND) - 1, 0),
                                             r, 0, part))

    qi = lax.broadcasted_iota(jnp.int32, (2 * BAND, 2 * BAND), 0) % BAND
    ki = lax.broadcasted_iota(jnp.int32, (2 * BAND, 2 * BAND), 1)
    band = (ki >= qi) & (ki <= qi + BAND)
    masks = jnp.where(jnp.stack([band, band & (ki >= BAND)]), 0.0, NEG).astype(F32)

    out_spec = pl.BlockSpec((Q_ROWS // n, None, n, A_WIDTH), lambda b, r, i: (b * nb + i, r, 0, 0))
    out_shape = jax.ShapeDtypeStruct((pieces, d, n, A_WIDTH), F32)
    o, lse = pl.pallas_call(
        _band_kernel,
        grid=(BATCH, d, nb),
        in_specs=[pl.BlockSpec(masks.shape, lambda b, r, i: (0, 0, 0)),
                  own(0), prev(1), own(1), prev(2), own(2)],
        out_specs=[out_spec, out_spec],
        out_shape=[out_shape, out_shape],
        scratch_shapes=[pltpu.VMEM((units, 2 * BAND, 2 * BAND), F32),
                        pltpu.VMEM((units, 2 * BAND, 2 * BAND), BF16),
                        pltpu.VMEM((units, BAND, LANES), F32)],
        compiler_params=_params(3),
        name=f"band_attention_d{d}",
    )(masks, view, view, view, view, view)
    return o.reshape(BATCH * SEQ, A_WIDTH), lse.reshape(BATCH * SEQ, A_WIDTH)


def _sample_attn_kernel(qkv_ref, c0_ref, c1_ref, c2_ref, y_ref):
    n_rows = DEC_SEQ * N_SLOTS
    qkv = qkv_ref[...].astype(F32)
    row_h = lax.broadcasted_iota(jnp.int32, (n_rows, A_WIDTH), 0) % N_SLOTS
    col_h = lax.broadcasted_iota(jnp.int32, (n_rows, A_WIDTH), 1) // HEAD_DIM
    own_head = row_h == col_h
    pad = jnp.zeros((NEW_PAD - DEC_SEQ, A_WIDTH), F32)

    def reach(n_keys, offset, d):
        t = lax.broadcasted_iota(jnp.int32, (n_rows, n_keys), 0) // N_SLOTS
        back = t - lax.broadcasted_iota(jnp.int32, (n_rows, n_keys), 1) - offset
        return (back >= 0) & (back <= BAND * d) & ((back & (d - 1)) == 0)

    outs, lses = [], []
    for grp, c_ref in enumerate((c0_ref, c1_ref, c2_ref)):
        d, window = DIL_RATES[grp], DIL_WINDOWS[grp]
        q = qkv[:, grp * A_WIDTH:(grp + 1) * A_WIDTH]
        k_new = qkv[:, (N_GROUPS_A + grp) * A_WIDTH:(N_GROUPS_A + grp + 1) * A_WIDTH]
        v_new = qkv[:, (2 * N_GROUPS_A + grp) * A_WIDTH:(2 * N_GROUPS_A + grp + 1) * A_WIDTH]
        q_rep = jnp.concatenate(
            [jnp.broadcast_to(q[t:t + 1], (N_SLOTS, A_WIDTH)) for t in range(DEC_SEQ)], axis=0)
        q_bd = jnp.where(own_head, q_rep, 0.0).astype(BF16)
        k_new = jnp.concatenate([k_new, pad], axis=0).astype(BF16)
        v_new = jnp.concatenate([v_new, pad], axis=0).astype(BF16)
        s_c = jnp.where(reach(window, -window, d), _dot(q_bd, c_ref[0].astype(BF16)), NEG)
        s_n = jnp.where(reach(NEW_PAD, 0, d), _dot_nt(q_bd, k_new), NEG)
        mx = jnp.maximum(jnp.max(s_c, axis=-1, keepdims=True), jnp.max(s_n, axis=-1, keepdims=True))
        p_c = jnp.exp(s_c - mx)
        p_n = jnp.exp(s_n - mx)
        den = jnp.sum(p_c, axis=-1, keepdims=True) + jnp.sum(p_n, axis=-1, keepdims=True)
        o = (_dot_nt(p_c.astype(BF16), c_ref[1].astype(BF16)) + _dot(p_n.astype(BF16), v_new)) / den
        lse = jnp.broadcast_to(mx + jnp.log(den), (n_rows, A_WIDTH))
        o = jnp.where(own_head, o, 0.0)
        lse = jnp.where(own_head, lse, 0.0)
        outs.append(jnp.concatenate(
            [jnp.sum(o[t * N_SLOTS:(t + 1) * N_SLOTS], axis=0, keepdims=True) for t in range(DEC_SEQ)],
            axis=0))
        lses.append(jnp.concatenate(
            [jnp.sum(lse[t * N_SLOTS:(t + 1) * N_SLOTS], axis=0, keepdims=True) for t in range(DEC_SEQ)],
            axis=0))
    y_ref[...] = _merge(outs, lses)


def _sample_attention(qkv, caches, li):
    views = [jnp.transpose(c, (0, 1, 3, 4, 5, 2)).reshape(-1, DEC_BATCH, 2, A_WIDTH, c.shape[2])
             for c in caches]
    return pl.pallas_call(
        _sample_attn_kernel,
        grid=(DEC_BATCH,),
        in_specs=[pl.BlockSpec((None, DEC_SEQ, QKV_WIDTH), lambda b: (b, 0, 0))] + [
            pl.BlockSpec((None, None, 2, A_WIDTH, w), lambda b: (li, b, 0, 0, 0)) for w in DIL_WINDOWS],
        out_specs=pl.BlockSpec((None, DEC_SEQ, A_WIDTH), lambda b: (b, 0, 0)),
        out_shape=jax.ShapeDtypeStruct((DEC_BATCH, DEC_SEQ, A_WIDTH), F32),
        compiler_params=_params(1),
        name="sample_attention",
    )(qkv.reshape(DEC_BATCH, DEC_SEQ, QKV_WIDTH), *views)


def _proj_kernel(x_ref, y_ref, w_ref, out_ref):
    out_ref[...] = x_ref[...] + _dot(y_ref[...].astype(BF16), w_ref[...])


def _proj(x, y, w):
    m = x.shape[0]
    return pl.pallas_call(
        _proj_kernel,
        grid=(1,),
        in_specs=[pl.BlockSpec((m, D_MODEL), lambda i: (0, 0)),
                  pl.BlockSpec((m, A_WIDTH), lambda i: (0, 0)),
                  pl.BlockSpec((A_WIDTH, D_MODEL), lambda i: (0, 0))],
        out_specs=pl.BlockSpec((m, D_MODEL), lambda i: (0, 0)),
        out_shape=jax.ShapeDtypeStruct((m, D_MODEL), F32),
        compiler_params=_params(1),
        name="proj",
    )(x, y, w)


def _gelu(x):
    a0 = -2.0 * (2.0 / jnp.pi) ** 0.5
    return x / (1.0 + jnp.exp(x * (a0 + (a0 * 0.044715) * (x * x))))


def _gmlp_kernel(*refs, tm, sample, n_cast):
    x_ref, g_ref, wuv_ref, lng_ref, lnb_ref, ws_ref, bs_ref, wo_ref = refs[:8]
    cast_src = refs[8:8 + n_cast]
    refs = refs[8 + n_cast:]
    out_ref = refs[0]
    if sample:
        v_ref = refs[1]
    refs = refs[2:] if sample else refs[1:]
    cast_dst = refs[:n_cast]
    zv_ref, vn_ref, um_ref = refs[n_cast:]
    _cast_rows(cast_src, cast_dst)
    x = x_ref[...]
    h = _rms(x, g_ref[...]).astype(BF16)
    n_uv = D_V // UV_COLS

    tot = jnp.zeros((tm, 1), F32)
    for c in range(n_uv):
        lo = c * UV_COLS
        z = _gelu(_dot(h, wuv_ref[:, D_V + lo:D_V + lo + UV_COLS]))
        zv_ref[:, lo:lo + UV_COLS] = z
        tot = tot + jnp.sum(z, axis=-1, keepdims=True)
    mu = tot / D_V
    sq = jnp.zeros((tm, 1), F32)
    for c in range(n_uv):
        zc = zv_ref[:, c * UV_COLS:(c + 1) * UV_COLS] - mu
        sq = sq + jnp.sum(zc * zc, axis=-1, keepdims=True)
    rstd = lax.rsqrt(sq / D_V + LN_EPS)
    for c in range(n_uv):
        cols = slice(c * UV_COLS, (c + 1) * UV_COLS)
        vn = (zv_ref[:, cols] - mu) * rstd * lng_ref[:, cols] + lnb_ref[:, cols]
        vn_ref[:, cols] = vn.astype(BF16)
        if sample:
            v_ref[:, cols] = vn

    ri = lax.broadcasted_iota(jnp.int32, (CHUNK, CHUNK), 0)
    ci = lax.broadcasted_iota(jnp.int32, (CHUNK, CHUNK), 1)
    causal = ci <= ri
    if sample:
        causal = causal & ((ri // DEC_SEQ) == (ci // DEC_SEQ))
    groups_per_mm = UV_COLS // GROUP_B
    for c in range(n_uv):
        u = _gelu(_dot(h, wuv_ref[:, c * UV_COLS:(c + 1) * UV_COLS]))
        for gl in range(groups_per_mm):
            grp = c * groups_per_mm + gl
            w = jnp.where(causal, ws_ref[grp], 0.0).astype(BF16)
            bias = bs_ref[:, grp:grp + 1]
            cols = slice(grp * GROUP_B, (grp + 1) * GROUP_B)
            for n in range(tm // CHUNK):
                rows = slice(n * CHUNK, (n + 1) * CHUNK)
                mixed = _dot(w, vn_ref[rows, cols]) + bias
                um_ref[rows, cols] = (u[rows, gl * GROUP_B:(gl + 1) * GROUP_B] * mixed).astype(BF16)
    out_ref[...] = x + _dot(um_ref[...], wo_ref[...])


def _gmlp(x, layer, g, w_uv, ln_g, ln_b, w_s, b_s, w_out, *, tm, sample, casts=()):
    m = x.shape[0]
    li = layer // 2
    steps = m // tm
    plans = [_cast_plan(src, lyr, steps) for src, lyr in casts]
    out_specs = [pl.BlockSpec((tm, D_MODEL), lambda i: (i, 0))]
    out_shape = [jax.ShapeDtypeStruct((m, D_MODEL), F32)]
    if sample:
        out_specs.append(pl.BlockSpec((tm, D_V), lambda i: (i, 0)))
        out_shape.append(jax.ShapeDtypeStruct((m, D_V), F32))
    n_main = len(out_specs)
    res = pl.pallas_call(
        functools.partial(_gmlp_kernel, tm=tm, sample=sample, n_cast=len(plans)),
        grid=(steps,),
        in_specs=[
            pl.BlockSpec((tm, D_MODEL), lambda i: (i, 0)),
            pl.BlockSpec((None, 1, D_MODEL), lambda i: (layer, 0, 0)),
            _resident((D_MODEL, 2 * D_V), lambda i: (0, 0)),
            pl.BlockSpec((None, 1, D_V), lambda i: (li, 0, 0)),
            pl.BlockSpec((None, 1, D_V), lambda i: (li, 0, 0)),
            pl.BlockSpec((None, N_GROUPS_B, CHUNK, CHUNK), lambda i: (li, 0, 0, 0)),
            pl.BlockSpec((None, CHUNK, N_GROUPS_B), lambda i: (li, 0, 0)),
            _resident((D_V, D_MODEL), lambda i: (0, 0)),
        ] + [p[0] for p in plans],
        out_specs=out_specs + [p[1] for p in plans],
        out_shape=out_shape + [p[2] for p in plans],
        scratch_shapes=[pltpu.VMEM((tm, D_V), F32), pltpu.VMEM((tm, D_V), BF16),
                        pltpu.VMEM((tm, D_V), BF16)],
        compiler_params=_params(1),
        name="gmlp",
    )(x, g, w_uv, ln_g, ln_b, w_s, b_s, w_out, *[src for src, _ in casts])
    return res[:n_main], res[n_main:]


def kernel(x_prompt, x_sample, cache_kv_w128, cache_kv_w512, cache_kv_w2048, norm_ffn1, w_ffn1_in,
           w_ffn1_out, norm_mix, norm_ffn2, w_ffn2_in, w_ffn2_out, w_qkv_a, w_out_a, w_uv_b,
           ln_v_gain, ln_v_bias, w_spatial, b_spatial, w_out_b, norm_final):
    caches = (cache_kv_w128, cache_kv_w512, cache_kv_w2048)
    mp, ms = BATCH * SEQ, DEC_BATCH * DEC_SEQ
    xp = x_prompt.reshape(mp, D_MODEL)
    xs = x_sample.reshape(ms, D_MODEL)

    g1 = norm_ffn1.reshape(DEPTH, 1, D_MODEL)
    gm = norm_mix.reshape(DEPTH, 1, D_MODEL)
    g2 = norm_ffn2.reshape(DEPTH, 1, D_MODEL)
    gf = norm_final.reshape(1, D_MODEL)
    lng = ln_v_gain.reshape(-1, 1, D_V)
    lnb = ln_v_bias.reshape(-1, 1, D_V)
    reps = CHUNK // DEC_SEQ
    ws_p = w_spatial
    bs_p = jnp.swapaxes(b_spatial, 1, 2)
    ws_s = jnp.tile(w_spatial[:, :, :DEC_SEQ, :DEC_SEQ], (1, 1, reps, reps))
    bs_s = jnp.swapaxes(jnp.tile(b_spatial[:, :, :DEC_SEQ], (1, 1, reps)), 1, 2)

    f32_weights = {"ffn1_in": w_ffn1_in, "ffn1_out": w_ffn1_out, "ffn2_in": w_ffn2_in,
                   "ffn2_out": w_ffn2_out, "qkv": w_qkv_a, "out_a": w_out_a, "uv": w_uv_b,
                   "out_b": w_out_b}
    bf16_weights = {("ffn1_in", 0): w_ffn1_in[0].astype(BF16),
                    ("ffn1_out", 0): w_ffn1_out[0].astype(BF16)}

    def jobs(*keys):
        return keys, tuple((f32_weights[name], idx) for name, idx in keys)

    def done(keys, converted):
        bf16_weights.update(zip(keys, converted, strict=True))

    def w(name, idx):
        return bf16_weights[(name, idx)]

    windows, kv_s, v_rows = None, [], []
    for i in range(DEPTH):
        li = i // 2
        attention = i % 2 == 0
        last = i == DEPTH - 1

        if not attention:
            keys, casts = jobs(("uv", li), ("out_b", li), ("ffn2_in", i), ("ffn2_out", i))
        elif i == 0:
            keys, casts = jobs(("qkv", li), ("out_a", li), ("ffn2_in", i), ("ffn2_out", i))
        else:
            keys, casts = jobs(("ffn2_in", i), ("ffn2_out", i))
        xp, converted, xs = _ffn(xp, i, g1, w("ffn1_in", i), w("ffn1_out", i), gf, tm=TILE,
                                 casts=casts, side=xs)
        done(keys, converted)

        keys, casts = jobs() if last else jobs(("ffn1_in", i + 1), ("ffn1_out", i + 1))
        if attention:
            qkv_g, windows, converted = _qkv_prompt(xp, i, gm, w("qkv", li), windows, casts)
            done(keys, converted)
            parts = [_band_attention(qkv_g[grp], grp) for grp in range(N_GROUPS_A)]
            mixer = ([p[0] for p in parts], [p[1] for p in parts], w("out_a", li))
            qkv, kv = _qkv_sample(xs, i, gm, w("qkv", li))
            kv_s.append(kv)
            y = _sample_attention(qkv, caches, li)
            xs = _proj(xs, y.reshape(ms, A_WIDTH), w("out_a", li))
            xp, _, xs = _ffn(xp, i, g2, w("ffn2_in", i), w("ffn2_out", i), gf, tm=TILE, final=last,
                             mixer=mixer, side=xs)
        else:
            (xp,), converted = _gmlp(xp, i, gm, w("uv", li), lng, lnb, ws_p, bs_p, w("out_b", li),
                                     tm=TILE, sample=False, casts=casts)
            done(keys, converted)
            (xs, v), _ = _gmlp(xs, i, gm, w("uv", li), lng, lnb, ws_s, bs_s, w("out_b", li),
                               tm=ms, sample=True)
            v_rows.append(v)
            keys, casts = jobs() if last else jobs(("qkv", li + 1), ("out_a", li + 1))
            xp, converted, xs = _ffn(xp, i, g2, w("ffn2_in", i), w("ffn2_out", i), gf, tm=TILE,
                                     final=last, casts=casts, side=xs)
            done(keys, converted)

    def prompt_window(grp):
        return jnp.transpose(windows[grp], (0, 1, 5, 2, 3, 4))

    def sample_rows(grp):
        return jnp.stack([kv.reshape(DEC_BATCH, DEC_SEQ, N_GROUPS_A, 2, N_SLOTS, HEAD_DIM)[:, :, grp]
                          for kv in kv_s])

    return (xp.reshape(BATCH, SEQ, D_MODEL), xs.reshape(DEC_BATCH, DEC_SEQ, D_MODEL),
            prompt_window(0), prompt_window(1), prompt_window(2),
            sample_rows(0), sample_rows(1), sample_rows(2),
            jnp.stack(v_rows).reshape(len(v_rows), DEC_BATCH, DEC_SEQ, D_V))
```

```python
import functools

import jax
import jax.numpy as jnp
from jax import lax
from jax.experimental import pallas as pl
from jax.experimental.pallas import tpu as pltpu

F32 = jnp.float32
BF16 = jnp.bfloat16

D_MODEL = 1024
BATCH = 4
SEQ = 4096
DEPTH = 4
DEC_BATCH = 32
DEC_SEQ = 4
HEAD_DIM = 64
N_SLOTS = 8
DIL_WINDOWS = (128, 512, 2048)
DIL_RATES = (1, 4, 16)
N_GROUPS_A = 3
A_WIDTH = N_SLOTS * HEAD_DIM
QKV_WIDTH = 3 * N_GROUPS_A * A_WIDTH
KV_WIDTH = 2 * N_GROUPS_A * A_WIDTH
BAND = 128
CHUNK = 128
D_V = 3072
N_GROUPS_B = 8
GROUP_B = D_V // N_GROUPS_B
D_FF = 2816
RMS_EPS = 1e-6
LN_EPS = 1e-5
NEG = -1e30
LOG2_E = 1.4426950408889634

VMEM_LIMIT_BYTES = 56 * 1024 * 1024
LANES = 128
BF16_ROWS = 16
FREE_STRIDE = 4
TILE = 512
TILES_PER_SEQ = SEQ // TILE
MM_COLS = 256
FF_COLS = MM_COLS
UV_COLS = 768
Q_ROWS = 256
NEW_PAD = 16


def _params(n_axes):
    return pltpu.CompilerParams(dimension_semantics=("arbitrary",) * n_axes,
                                vmem_limit_bytes=VMEM_LIMIT_BYTES)


def _resident(shape, index_map):
    return pl.BlockSpec(shape, index_map, pipeline_mode=pl.Buffered(1))


def _rms(x, g):
    return x * lax.rsqrt(jnp.mean(x * x, axis=-1, keepdims=True) + RMS_EPS) * g


def _dot(a, b):
    return jnp.dot(a, b, preferred_element_type=F32)


def _dot_nt(a, b):
    return lax.dot_general(a, b, (((1,), (1,)), ((), ())), preferred_element_type=F32)


def _cast_plan(src, layer, n_steps):
    rows, cols = src.shape[1:]
    rb = rows // n_steps
    if rows % n_steps or rb % BF16_ROWS:
        rb = LANES
    nb = rows // rb
    assert rows % rb == 0 and nb <= n_steps
    in_spec = pl.BlockSpec((None, rb, cols), lambda i: (layer, jnp.minimum(i, nb - 1), 0))
    out_spec = pl.BlockSpec((rb, cols), lambda i: (jnp.minimum(i, nb - 1), 0))
    return in_spec, out_spec, jax.ShapeDtypeStruct((rows, cols), BF16)


def _cast_rows(src_refs, dst_refs):
    for src, dst in zip(src_refs, dst_refs, strict=True):
        dst[...] = src[...].astype(BF16)


def _merge(outs, lses):
    mx = jnp.maximum(jnp.maximum(lses[0], lses[1]), lses[2])
    e = [jnp.exp(l - mx) for l in lses]
    den = e[0] + e[1] + e[2]
    return (e[0] / den) * outs[0] + (e[1] / den) * outs[1] + (e[2] / den) * outs[2]


def _position_order(slab_ref, slab2_ref, src_ref, d):
    n = TILE // d
    n_cols = A_WIDTH // LANES
    two_pass = d > FREE_STRIDE
    per = d // FREE_STRIDE if two_pass else d
    m = TILE // FREE_STRIDE
    dst = slab2_ref if two_pass else slab_ref
    for k in range(d):
        r0, r1 = divmod(k, per)
        for cc in range(n_cols):
            dst[cc, pl.ds(r0 * m * two_pass + r1, n, stride=per), :] = (
                src_ref[k * n:(k + 1) * n, cc * LANES:(cc + 1) * LANES])
    if two_pass:
        for r0 in range(FREE_STRIDE):
            for cc in range(n_cols):
                slab_ref[cc, pl.ds(r0, m, stride=FREE_STRIDE), :] = slab2_ref[cc, r0 * m:(r0 + 1) * m, :]
    return [slab_ref[cc] for cc in range(n_cols)]


def _merged_attention(o0, o1, o2, l0, l1, l2, slab_ref):
    cols = [slice(cc * LANES, (cc + 1) * LANES) for cc in range(A_WIDTH // LANES)]
    outs = [[o0[:, c] for c in cols],
            _position_order(slab_ref.at[0], None, o1, DIL_RATES[1]),
            _position_order(slab_ref.at[1], slab_ref.at[4], o2, DIL_RATES[2])]
    lses = [[l0[:, c] for c in cols],
            _position_order(slab_ref.at[2], None, l1, DIL_RATES[1]),
            _position_order(slab_ref.at[3], slab_ref.at[5], l2, DIL_RATES[2])]
    return jnp.concatenate(
        [_merge([o[cc] for o in outs], [l[cc] for l in lses]) for cc in range(len(cols))], axis=1)


def _ffn_kernel(*refs, final, mixer, n_cast, side):
    x_ref, g_ref, win_ref, wout_ref, gf_ref = refs[:5]
    refs = refs[5:]
    if mixer:
        *attn_refs, wmix_ref = refs[:7]
        refs = refs[7:]
    cast_src, refs = refs[:n_cast], refs[n_cast:]
    if side:
        xs_ref, refs = refs[0], refs[1:]
    o_ref, cast_dst, refs = refs[0], refs[1:n_cast + 1], refs[n_cast + 1:]
    if side:
        os_ref, refs = refs[0], refs[1:]
    a_ref = refs[0]

    def half_step(x):
        rows = x.shape[0]
        h = _rms(x, g_ref[...]).astype(BF16)
        for c in range(D_FF // FF_COLS):
            lo = c * FF_COLS
            gate = _dot(h, win_ref[:, lo:lo + FF_COLS])
            up = _dot(h, win_ref[:, D_FF + lo:D_FF + lo + FF_COLS])
            a_ref[:rows, lo:lo + FF_COLS] = (gate * jax.nn.sigmoid(gate) * up).astype(BF16)
        y = x + 0.5 * _dot(a_ref[:rows], wout_ref[...])
        return _rms(y, gf_ref[...]) if final else y

    _cast_rows(cast_src, cast_dst)
    x = x_ref[...]
    if mixer:
        x = x + _dot(_merged_attention(*attn_refs, refs[1]).astype(BF16), wmix_ref[...])
    o_ref[...] = half_step(x)
    if side:
        @pl.when(pl.program_id(0) == pl.num_programs(0) - 1)
        def _():
            os_ref[...] = half_step(xs_ref[...])


def _ffn(x, layer, g, w_in, w_out, g_final, *, tm, final=False, mixer=None, casts=(), side=None):
    m = x.shape[0]
    steps = m // tm
    plans = [_cast_plan(src, lyr, steps) for src, lyr in casts]
    in_specs = [
        pl.BlockSpec((tm, D_MODEL), lambda i: (i, 0)),
        pl.BlockSpec((None, 1, D_MODEL), lambda i: (layer, 0, 0)),
        _resident((D_MODEL, 2 * D_FF), lambda i: (0, 0)),
        _resident((D_FF, D_MODEL), lambda i: (0, 0)),
        pl.BlockSpec((1, D_MODEL), lambda i: (0, 0)),
    ]
    args = [x, g, w_in, w_out, g_final]
    scratch = [pltpu.VMEM((tm, D_FF), BF16)]
    if mixer is not None:
        outs, lses, w_mix = mixer
        in_specs += [pl.BlockSpec((tm, A_WIDTH), lambda i: (i, 0))] * 6
        in_specs.append(_resident((A_WIDTH, D_MODEL), lambda i: (0, 0)))
        args += [*outs, *lses, w_mix]
        scratch.append(pltpu.VMEM((6, A_WIDTH // LANES, TILE, LANES), F32))
    side_args, side_specs, side_shapes = [], [], []
    if side is not None:
        side_args = [side]
        side_specs = [pl.BlockSpec(side.shape, lambda i: (0, 0))]
        side_shapes = [jax.ShapeDtypeStruct(side.shape, F32)]
    res = pl.pallas_call(
        functools.partial(_ffn_kernel, final=final, mixer=mixer is not None, n_cast=len(plans),
                          side=side is not None),
        grid=(steps,),
        in_specs=in_specs + [p[0] for p in plans] + side_specs,
        out_specs=([pl.BlockSpec((tm, D_MODEL), lambda i: (i, 0))] + [p[1] for p in plans]
                   + side_specs),
        out_shape=[jax.ShapeDtypeStruct((m, D_MODEL), F32)] + [p[2] for p in plans] + side_shapes,
        scratch_shapes=scratch,
        compiler_params=_params(1),
        name="ffn",
    )(*args, *[src for src, _ in casts], *side_args)
    n = 1 + len(plans)
    return res[0], res[1:n], (res[n] if side is not None else None)


def _qkv_sample_kernel(x_ref, g_ref, w_ref, qkv_ref, kv_ref):
    h = _rms(x_ref[...], g_ref[...]).astype(BF16)
    for c in range(QKV_WIDTH // A_WIDTH):
        lo = c * A_WIDTH
        part, grp = divmod(c, N_GROUPS_A)
        y = _dot(h, w_ref[:, lo:lo + A_WIDTH])
        if part == 0:
            qkv_ref[:, lo:lo + A_WIDTH] = (y * (HEAD_DIM ** -0.5)).astype(BF16)
        else:
            qkv_ref[:, lo:lo + A_WIDTH] = y.astype(BF16)
            dst = (2 * grp + part - 1) * A_WIDTH
            kv_ref[:, dst:dst + A_WIDTH] = y


def _qkv_sample(x, layer, g, w):
    m = x.shape[0]
    return pl.pallas_call(
        _qkv_sample_kernel,
        grid=(1,),
        in_specs=[
            pl.BlockSpec((m, D_MODEL), lambda i: (0, 0)),
            pl.BlockSpec((None, 1, D_MODEL), lambda i: (layer, 0, 0)),
            pl.BlockSpec((D_MODEL, QKV_WIDTH), lambda i: (0, 0)),
        ],
        out_specs=[
            pl.BlockSpec((m, QKV_WIDTH), lambda i: (0, 0)),
            pl.BlockSpec((m, KV_WIDTH), lambda i: (0, 0)),
        ],
        out_shape=[jax.ShapeDtypeStruct((m, QKV_WIDTH), BF16),
                   jax.ShapeDtypeStruct((m, KV_WIDTH), F32)],
        compiler_params=_params(1),
        name="qkv_sample",
    )(x, g, w)


def _regroup_rows(slab_ref, slab2_ref, y, dst_ref, col0, d):
    n = TILE // d
    cols = [slice(cc * LANES, (cc + 1) * LANES) for cc in range(A_WIDTH // LANES)]
    for cc, c in enumerate(cols):
        slab_ref[cc] = y[:, c]
    if d > FREE_STRIDE:
        m = TILE // FREE_STRIDE
        for r0 in range(FREE_STRIDE):
            for cc in range(len(cols)):
                slab2_ref[cc, r0 * m:(r0 + 1) * m, :] = slab_ref[cc, pl.ds(r0, m, stride=FREE_STRIDE), :]
        src, per, step = slab2_ref, d // FREE_STRIDE, m
    else:
        src, per, step = slab_ref, d, 0
    for k in range(d):
        r0, r1 = divmod(k, per)
        for cc, c in enumerate(cols):
            dst_ref[k * n:(k + 1) * n, col0 + c.start:col0 + c.stop] = (
                src[cc, pl.ds(r0 * step + r1, n, stride=per), :].astype(BF16))


def _qkv_prompt_kernel(*refs, n_cast, n_prev):
    x_ref, g_ref, w_ref = refs[:3]
    cast_src = refs[3 + n_prev:3 + n_prev + n_cast]
    outs = refs[3 + n_prev + n_cast:]
    dst, win = outs[:N_GROUPS_A], outs[N_GROUPS_A:2 * N_GROUPS_A]
    cast_dst = outs[2 * N_GROUPS_A:2 * N_GROUPS_A + n_cast]
    slab_ref = outs[2 * N_GROUPS_A + n_cast]
    _cast_rows(cast_src, cast_dst)
    h = _rms(x_ref[...], g_ref[...]).astype(BF16)
    for c in range(QKV_WIDTH // A_WIDTH):
        part, grp = divmod(c, N_GROUPS_A)
        y = _dot(h, w_ref[:, c * A_WIDTH:(c + 1) * A_WIDTH])
        if part == 0:
            y = y * (HEAD_DIM ** -0.5 * LOG2_E)
        col0 = part * A_WIDTH
        if grp == 0:
            dst[0][:, col0:col0 + A_WIDTH] = y.astype(BF16)
        else:
            _regroup_rows(slab_ref.at[(grp - 1) * 3 + part], slab_ref.at[6 + part], y, dst[grp],
                          col0, DIL_RATES[grp])
        if part > 0:
            keep = min(DIL_WINDOWS[grp], TILE)
            win[grp][part - 1] = y[TILE - keep:].T.reshape(N_SLOTS, HEAD_DIM, keep)


def _qkv_prompt(x, layer, g, w, windows=None, casts=()):
    m = x.shape[0]
    li = layer // 2
    steps = m // TILE
    plans = [_cast_plan(src, lyr, steps) for src, lyr in casts]
    row_spec = pl.BlockSpec((TILE, 3 * A_WIDTH), lambda i: (i, 0))
    win_specs, win_shapes = [], []
    for grp in range(N_GROUPS_A):
        keep = min(DIL_WINDOWS[grp], TILE)
        first_tile = TILES_PER_SEQ - max(DIL_WINDOWS[grp] // TILE, 1)
        win_specs.append(pl.BlockSpec(
            (None, None, 2, N_SLOTS, HEAD_DIM, keep),
            lambda i, ft=first_tile: (li, i // TILES_PER_SEQ, 0, 0, 0,
                                      jnp.maximum(i % TILES_PER_SEQ - ft, 0))))
        win_shapes.append(jax.ShapeDtypeStruct(
            (DEPTH // 2, BATCH, 2, N_SLOTS, HEAD_DIM, DIL_WINDOWS[grp]), F32))
    prev = list(windows) if windows is not None else []
    n_fixed = 3
    res = pl.pallas_call(
        functools.partial(_qkv_prompt_kernel, n_cast=len(plans), n_prev=len(prev)),
        grid=(steps,),
        in_specs=[
            pl.BlockSpec((TILE, D_MODEL), lambda i: (i, 0)),
            pl.BlockSpec((None, 1, D_MODEL), lambda i: (layer, 0, 0)),
            _resident((D_MODEL, QKV_WIDTH), lambda i: (0, 0)),
        ] + [pl.BlockSpec(memory_space=pl.ANY)] * len(prev) + [p[0] for p in plans],
        out_specs=[row_spec] * N_GROUPS_A + win_specs + [p[1] for p in plans],
        out_shape=([jax.ShapeDtypeStruct((m, 3 * A_WIDTH), BF16)] * N_GROUPS_A + win_shapes
                   + [p[2] for p in plans]),
        input_output_aliases={n_fixed + k: N_GROUPS_A + k for k in range(len(prev))},
        scratch_shapes=[pltpu.VMEM((9, A_WIDTH // LANES, TILE, LANES), F32)],
        compiler_params=_params(1),
        name="qkv_prompt",
    )(x, g, w, *prev, *[src for src, _ in casts])
    return res[:N_GROUPS_A], res[N_GROUPS_A:2 * N_GROUPS_A], res[2 * N_GROUPS_A:]


def _band_kernel(mask_ref, q_ref, kp_ref, ko_ref, vp_ref, vo_ref, o_ref, l_ref, s_ref, p_ref, m_ref):
    n = q_ref.shape[1]
    n_sub, n_pair = Q_ROWS // BAND, A_WIDTH // LANES
    bias = mask_ref[0]
    bias_first = mask_ref[(pl.program_id(2) == 0).astype(jnp.int32)]
    low = lax.broadcasted_iota(jnp.int32, (1, LANES), 1) < HEAD_DIM
    q_all = q_ref[...].reshape(Q_ROWS, A_WIDTH)
    ko_all = ko_ref[...].reshape(Q_ROWS, A_WIDTH)
    vo_all = vo_ref[...].reshape(Q_ROWS, A_WIDTH)
    kp_all = kp_ref[...].reshape(BAND, A_WIDTH)
    vp_all = vp_ref[...].reshape(BAND, A_WIDTH)

    def keys(prev, own, j, cs):
        if j == 0:
            return jnp.concatenate([prev[:, cs], own[:BAND, cs]], axis=0)
        return own[:, cs]

    for pr in range(n_pair):
        cs = slice(pr * LANES, (pr + 1) * LANES)
        for j in range(n_sub):
            q = q_all[j * BAND:(j + 1) * BAND, cs]
            zero = jnp.zeros_like(q)
            q_ab = jnp.concatenate([jnp.where(low, q, zero), jnp.where(low, zero, q)], axis=0)
            s = _dot_nt(q_ab, keys(kp_all, ko_all, j, cs))
            s_ref[pr * n_sub + j] = s + (bias_first if j == 0 else bias)

    for u in range(n_pair * n_sub):
        mx = jnp.max(s_ref[u], axis=-1, keepdims=True)
        p_ref[u] = jnp.exp2(s_ref[u] - mx).astype(BF16)
        m_ref[u] = jnp.where(low, mx[:BAND], mx[BAND:])

    one = jnp.ones((2 * BAND, LANES), BF16)
    for pr in range(n_pair):
        cs = slice(pr * LANES, (pr + 1) * LANES)
        for j in range(n_sub):
            u = pr * n_sub + j
            vv = keys(vp_all, vo_all, j, cs)
            oa = _dot(p_ref[u, :BAND], jnp.where(low, vv, one))
            ob = _dot(p_ref[u, BAND:], jnp.where(low, one, vv))
            den = pltpu.roll(jnp.where(low, ob, oa), HEAD_DIM, axis=1)
            lse = (m_ref[u] + jnp.log2(den)) * (1.0 / LOG2_E)
            tiles = slice(j * (BAND // n), (j + 1) * (BAND // n))
            o_ref[tiles, :, cs] = (jnp.where(low, oa, ob) / den).reshape(BAND // n, n, LANES)
            l_ref[tiles, :, cs] = lse.reshape(BAND // n, n, LANES)


def _band_attention(qkv, grp):
    d = DIL_RATES[grp]
    n = min(TILE // d, BAND)
    pieces = BATCH * SEQ // (d * n)
    nb = SEQ // (d * Q_ROWS)
    units = (Q_ROWS // BAND) * (A_WIDTH // LANES)
    view = qkv.reshape(pieces, d, n, 3 * A_WIDTH)

    def own(part):
        return pl.BlockSpec((Q_ROWS // n, None, n, A_WIDTH), lambda b, r, i: (b * nb + i, r, 0, part))

    def prev(part):
        return pl.BlockSpec((BAND // n, None, n, A_WIDTH),
                            lambda b, r, i: (jnp.maximum((b * nb + i) * (Q_ROWS // BAND) - 1, 0),
                                             r, 0, part))

    qi = lax.broadcasted_iota(jnp.int32, (2 * BAND, 2 * BAND), 0) % BAND
    ki = lax.broadcasted_iota(jnp.int32, (2 * BAND, 2 * BAND), 1)
    band = (ki >= qi) & (ki <= qi + BAND)
    masks = jnp.where(jnp.stack([band, band & (ki >= BAND)]), 0.0, NEG).astype(F32)

    out_spec = pl.BlockSpec((Q_ROWS // n, None, n, A_WIDTH), lambda b, r, i: (b * nb + i, r, 0, 0))
    out_shape = jax.ShapeDtypeStruct((pieces, d, n, A_WIDTH), F32)
    o, lse = pl.pallas_call(
        _band_kernel,
        grid=(BATCH, d, nb),
        in_specs=[pl.BlockSpec(masks.shape, lambda b, r, i: (0, 0, 0)),
                  own(0), prev(1), own(1), prev(2), own(2)],
        out_specs=[out_spec, out_spec],
        out_shape=[out_shape, out_shape],
        scratch_shapes=[pltpu.VMEM((units, 2 * BAND, 2 * BAND), F32),
                        pltpu.VMEM((units, 2 * BAND, 2 * BAND), BF16),
                        pltpu.VMEM((units, BAND, LANES), F32)],
        compiler_params=_params(3),
        name=f"band_attention_d{d}",
    )(masks, view, view, view, view, view)
    return o.reshape(BATCH * SEQ, A_WIDTH), lse.reshape(BATCH * SEQ, A_WIDTH)


def _sample_attn_kernel(qkv_ref, c0_ref, c1_ref, c2_ref, y_ref):
    n_rows = DEC_SEQ * N_SLOTS
    qkv = qkv_ref[...].astype(F32)
    row_h = lax.broadcasted_iota(jnp.int32, (n_rows, A_WIDTH), 0) % N_SLOTS
    col_h = lax.broadcasted_iota(jnp.int32, (n_rows, A_WIDTH), 1) // HEAD_DIM
    own_head = row_h == col_h
    pad = jnp.zeros((NEW_PAD - DEC_SEQ, A_WIDTH), F32)

    def reach(n_keys, offset, d):
        t = lax.broadcasted_iota(jnp.int32, (n_rows, n_keys), 0) // N_SLOTS
        back = t - lax.broadcasted_iota(jnp.int32, (n_rows, n_keys), 1) - offset
        return (back >= 0) & (back <= BAND * d) & ((back & (d - 1)) == 0)

    outs, lses = [], []
    for grp, c_ref in enumerate((c0_ref, c1_ref, c2_ref)):
        d, window = DIL_RATES[grp], DIL_WINDOWS[grp]
        q = qkv[:, grp * A_WIDTH:(grp + 1) * A_WIDTH]
        k_new = qkv[:, (N_GROUPS_A + grp) * A_WIDTH:(N_GROUPS_A + grp + 1) * A_WIDTH]
        v_new = qkv[:, (2 * N_GROUPS_A + grp) * A_WIDTH:(2 * N_GROUPS_A + grp + 1) * A_WIDTH]
        q_rep = jnp.concatenate(
            [jnp.broadcast_to(q[t:t + 1], (N_SLOTS, A_WIDTH)) for t in range(DEC_SEQ)], axis=0)
        q_bd = jnp.where(own_head, q_rep, 0.0).astype(BF16)
        k_new = jnp.concatenate([k_new, pad], axis=0).astype(BF16)
        v_new = jnp.concatenate([v_new, pad], axis=0).astype(BF16)
        s_c = jnp.where(reach(window, -window, d), _dot(q_bd, c_ref[0].astype(BF16)), NEG)
        s_n = jnp.where(reach(NEW_PAD, 0, d), _dot_nt(q_bd, k_new), NEG)
        mx = jnp.maximum(jnp.max(s_c, axis=-1, keepdims=True), jnp.max(s_n, axis=-1, keepdims=True))
        p_c = jnp.exp(s_c - mx)
        p_n = jnp.exp(s_n - mx)
        den = jnp.sum(p_c, axis=-1, keepdims=True) + jnp.sum(p_n, axis=-1, keepdims=True)
        o = (_dot_nt(p_c.astype(BF16), c_ref[1].astype(BF16)) + _dot(p_n.astype(BF16), v_new)) / den
        lse = jnp.broadcast_to(mx + jnp.log(den), (n_rows, A_WIDTH))
        o = jnp.where(own_head, o, 0.0)
        lse = jnp.where(own_head, lse, 0.0)
        outs.append(jnp.concatenate(
            [jnp.sum(o[t * N_SLOTS:(t + 1) * N_SLOTS], axis=0, keepdims=True) for t in range(DEC_SEQ)],
            axis=0))
        lses.append(jnp.concatenate(
            [jnp.sum(lse[t * N_SLOTS:(t + 1) * N_SLOTS], axis=0, keepdims=True) for t in range(DEC_SEQ)],
            axis=0))
    y_ref[...] = _merge(outs, lses)


def _sample_attention(qkv, caches, li):
    views = [jnp.transpose(c, (0, 1, 3, 4, 5, 2)).reshape(-1, DEC_BATCH, 2, A_WIDTH, c.shape[2])
             for c in caches]
    return pl.pallas_call(
        _sample_attn_kernel,
        grid=(DEC_BATCH,),
        in_specs=[pl.BlockSpec((None, DEC_SEQ, QKV_WIDTH), lambda b: (b, 0, 0))] + [
            pl.BlockSpec((None, None, 2, A_WIDTH, w), lambda b: (li, b, 0, 0, 0)) for w in DIL_WINDOWS],
        out_specs=pl.BlockSpec((None, DEC_SEQ, A_WIDTH), lambda b: (b, 0, 0)),
        out_shape=jax.ShapeDtypeStruct((DEC_BATCH, DEC_SEQ, A_WIDTH), F32),
        compiler_params=_params(1),
        name="sample_attention",
    )(qkv.reshape(DEC_BATCH, DEC_SEQ, QKV_WIDTH), *views)


def _proj_kernel(x_ref, y_ref, w_ref, out_ref):
    out_ref[...] = x_ref[...] + _dot(y_ref[...].astype(BF16), w_ref[...])


def _proj(x, y, w):
    m = x.shape[0]
    return pl.pallas_call(
        _proj_kernel,
        grid=(1,),
        in_specs=[pl.BlockSpec((m, D_MODEL), lambda i: (0, 0)),
                  pl.BlockSpec((m, A_WIDTH), lambda i: (0, 0)),
                  pl.BlockSpec((A_WIDTH, D_MODEL), lambda i: (0, 0))],
        out_specs=pl.BlockSpec((m, D_MODEL), lambda i: (0, 0)),
        out_shape=jax.ShapeDtypeStruct((m, D_MODEL), F32),
        compiler_params=_params(1),
        name="proj",
    )(x, y, w)


def _gelu(x):
    a0 = -2.0 * (2.0 / jnp.pi) ** 0.5 * LOG2_E
    return x / (1.0 + jnp.exp2(x * (a0 + (a0 * 0.044715) * (x * x))))


def _gmlp_kernel(*refs, tm, sample, n_cast):
    x_ref, g_ref, wuv_ref, lng_ref, lnb_ref, ws_ref, bs_ref, wo_ref = refs[:8]
    cast_src = refs[8:8 + n_cast]
    refs = refs[8 + n_cast:]
    out_ref = refs[0]
    if sample:
        v_ref = refs[1]
    refs = refs[2:] if sample else refs[1:]
    cast_dst = refs[:n_cast]
    zv_ref, vn_ref, um_ref = refs[n_cast:]
    _cast_rows(cast_src, cast_dst)
    x = x_ref[...]
    h = _rms(x, g_ref[...]).astype(BF16)
    n_uv = D_V // UV_COLS

    tot = jnp.zeros((tm, 1), F32)
    for c in range(n_uv):
        lo = c * UV_COLS
        z = _gelu(_dot(h, wuv_ref[:, D_V + lo:D_V + lo + UV_COLS]))
        zv_ref[:, lo:lo + UV_COLS] = z
        tot = tot + jnp.sum(z, axis=-1, keepdims=True)
    mu = tot / D_V
    sq = jnp.zeros((tm, 1), F32)
    for c in range(n_uv):
        zc = zv_ref[:, c * UV_COLS:(c + 1) * UV_COLS] - mu
        sq = sq + jnp.sum(zc * zc, axis=-1, keepdims=True)
    rstd = lax.rsqrt(sq / D_V + LN_EPS)
    for c in range(n_uv):
        cols = slice(c * UV_COLS, (c + 1) * UV_COLS)
        vn = (zv_ref[:, cols] - mu) * rstd * lng_ref[:, cols] + lnb_ref[:, cols]
        vn_ref[:, cols] = vn.astype(BF16)
        if sample:
            v_ref[:, cols] = vn

    ri = lax.broadcasted_iota(jnp.int32, (CHUNK, CHUNK), 0)
    ci = lax.broadcasted_iota(jnp.int32, (CHUNK, CHUNK), 1)
    causal = ci <= ri
    if sample:
        causal = causal & ((ri // DEC_SEQ) == (ci // DEC_SEQ))
    groups_per_mm = UV_COLS // GROUP_B
    for c in range(n_uv):
        u = _gelu(_dot(h, wuv_ref[:, c * UV_COLS:(c + 1) * UV_COLS]))
        for gl in range(groups_per_mm):
            grp = c * groups_per_mm + gl
            w = jnp.where(causal, ws_ref[grp], 0.0).astype(BF16)
            bias = bs_ref[:, grp:grp + 1]
            cols = slice(grp * GROUP_B, (grp + 1) * GROUP_B)
            for n in range(tm // CHUNK):
                rows = slice(n * CHUNK, (n + 1) * CHUNK)
                mixed = _dot(w, vn_ref[rows, cols]) + bias
                um_ref[rows, cols] = (u[rows, gl * GROUP_B:(gl + 1) * GROUP_B] * mixed).astype(BF16)
    out_ref[...] = x + _dot(um_ref[...], wo_ref[...])


def _gmlp(x, layer, g, w_uv, ln_g, ln_b, w_s, b_s, w_out, *, tm, sample, casts=()):
    m = x.shape[0]
    li = layer // 2
    steps = m // tm
    plans = [_cast_plan(src, lyr, steps) for src, lyr in casts]
    out_specs = [pl.BlockSpec((tm, D_MODEL), lambda i: (i, 0))]
    out_shape = [jax.ShapeDtypeStruct((m, D_MODEL), F32)]
    if sample:
        out_specs.append(pl.BlockSpec((tm, D_V), lambda i: (i, 0)))
        out_shape.append(jax.ShapeDtypeStruct((m, D_V), F32))
    n_main = len(out_specs)
    res = pl.pallas_call(
        functools.partial(_gmlp_kernel, tm=tm, sample=sample, n_cast=len(plans)),
        grid=(steps,),
        in_specs=[
            pl.BlockSpec((tm, D_MODEL), lambda i: (i, 0)),
            pl.BlockSpec((None, 1, D_MODEL), lambda i: (layer, 0, 0)),
            _resident((D_MODEL, 2 * D_V), lambda i: (0, 0)),
            pl.BlockSpec((None, 1, D_V), lambda i: (li, 0, 0)),
            pl.BlockSpec((None, 1, D_V), lambda i: (li, 0, 0)),
            pl.BlockSpec((None, N_GROUPS_B, CHUNK, CHUNK), lambda i: (li, 0, 0, 0)),
            pl.BlockSpec((None, CHUNK, N_GROUPS_B), lambda i: (li, 0, 0)),
            _resident((D_V, D_MODEL), lambda i: (0, 0)),
        ] + [p[0] for p in plans],
        out_specs=out_specs + [p[1] for p in plans],
        out_shape=out_shape + [p[2] for p in plans],
        scratch_shapes=[pltpu.VMEM((tm, D_V), F32), pltpu.VMEM((tm, D_V), BF16),
                        pltpu.VMEM((tm, D_V), BF16)],
        compiler_params=_params(1),
        name="gmlp",
    )(x, g, w_uv, ln_g, ln_b, w_s, b_s, w_out, *[src for src, _ in casts])
    return res[:n_main], res[n_main:]


def kernel(x_prompt, x_sample, cache_kv_w128, cache_kv_w512, cache_kv_w2048, norm_ffn1, w_ffn1_in,
           w_ffn1_out, norm_mix, norm_ffn2, w_ffn2_in, w_ffn2_out, w_qkv_a, w_out_a, w_uv_b,
           ln_v_gain, ln_v_bias, w_spatial, b_spatial, w_out_b, norm_final):
    caches = (cache_kv_w128, cache_kv_w512, cache_kv_w2048)
    mp, ms = BATCH * SEQ, DEC_BATCH * DEC_SEQ
    xp = x_prompt.reshape(mp, D_MODEL)
    xs = x_sample.reshape(ms, D_MODEL)

    g1 = norm_ffn1.reshape(DEPTH, 1, D_MODEL)
    gm = norm_mix.reshape(DEPTH, 1, D_MODEL)
    g2 = norm_ffn2.reshape(DEPTH, 1, D_MODEL)
    gf = norm_final.reshape(1, D_MODEL)
    lng = ln_v_gain.reshape(-1, 1, D_V)
    lnb = ln_v_bias.reshape(-1, 1, D_V)
    reps = CHUNK // DEC_SEQ
    ws_p = w_spatial
    bs_p = jnp.swapaxes(b_spatial, 1, 2)
    ws_s = jnp.tile(w_spatial[:, :, :DEC_SEQ, :DEC_SEQ], (1, 1, reps, reps))
    bs_s = jnp.swapaxes(jnp.tile(b_spatial[:, :, :DEC_SEQ], (1, 1, reps)), 1, 2)

    f32_weights = {"ffn1_in": w_ffn1_in, "ffn1_out": w_ffn1_out, "ffn2_in": w_ffn2_in,
                   "ffn2_out": w_ffn2_out, "qkv": w_qkv_a, "out_a": w_out_a, "uv": w_uv_b,
                   "out_b": w_out_b}
    bf16_weights = {("ffn1_in", 0): w_ffn1_in[0].astype(BF16),
                    ("ffn1_out", 0): w_ffn1_out[0].astype(BF16)}

    def jobs(*keys):
        return keys, tuple((f32_weights[name], idx) for name, idx in keys)

    def done(keys, converted):
        bf16_weights.update(zip(keys, converted, strict=True))

    def w(name, idx):
        return bf16_weights[(name, idx)]

    windows, kv_s, v_rows = None, [], []
    for i in range(DEPTH):
        li = i // 2
        attention = i % 2 == 0
        last = i == DEPTH - 1

        if not attention:
            keys, casts = jobs(("uv", li), ("out_b", li), ("ffn2_in", i), ("ffn2_out", i))
        elif i == 0:
            keys, casts = jobs(("qkv", li), ("out_a", li), ("ffn2_in", i), ("ffn2_out", i))
        else:
            keys, casts = jobs(("ffn2_in", i), ("ffn2_out", i))
        xp, converted, xs = _ffn(xp, i, g1, w("ffn1_in", i), w("ffn1_out", i), gf, tm=TILE,
                                 casts=casts, side=xs)
        done(keys, converted)

        keys, casts = jobs() if last else jobs(("ffn1_in", i + 1), ("ffn1_out", i + 1))
        if attention:
            qkv_g, windows, converted = _qkv_prompt(xp, i, gm, w("qkv", li), windows, casts)
            done(keys, converted)
            parts = [_band_attention(qkv_g[grp], grp) for grp in range(N_GROUPS_A)]
            mixer = ([p[0] for p in parts], [p[1] for p in parts], w("out_a", li))
            qkv, kv = _qkv_sample(xs, i, gm, w("qkv", li))
            kv_s.append(kv)
            y = _sample_attention(qkv, caches, li)
            xs = _proj(xs, y.reshape(ms, A_WIDTH), w("out_a", li))
            xp, _, xs = _ffn(xp, i, g2, w("ffn2_in", i), w("ffn2_out", i), gf, tm=TILE, final=last,
                             mixer=mixer, side=xs)
        else:
            (xp,), converted = _gmlp(xp, i, gm, w("uv", li), lng, lnb, ws_p, bs_p, w("out_b", li),
                                     tm=TILE, sample=False, casts=casts)
            done(keys, converted)
            (xs, v), _ = _gmlp(xs, i, gm, w("uv", li), lng, lnb, ws_s, bs_s, w("out_b", li),
                               tm=ms, sample=True)
            v_rows.append(v)
            keys, casts = jobs() if last else jobs(("qkv", li + 1), ("out_a", li + 1))
            xp, converted, xs = _ffn(xp, i, g2, w("ffn2_in", i), w("ffn2_out", i), gf, tm=TILE,
                                     final=last, casts=casts, side=xs)
            done(keys, converted)

    def prompt_window(grp):
        return jnp.transpose(windows[grp], (0, 1, 5, 2, 3, 4))

    def sample_rows(grp):
        return jnp.stack([kv.reshape(DEC_BATCH, DEC_SEQ, N_GROUPS_A, 2, N_SLOTS, HEAD_DIM)[:, :, grp]
                          for kv in kv_s])

    return (xp.reshape(BATCH, SEQ, D_MODEL), xs.reshape(DEC_BATCH, DEC_SEQ, D_MODEL),
            prompt_window(0), prompt_window(1), prompt_window(2),
            sample_rows(0), sample_rows(1), sample_rows(2),
            jnp.stack(v_rows).reshape(len(v_rows), DEC_BATCH, DEC_SEQ, D_V))
```

```python
import functools

import jax
import jax.numpy as jnp
from jax import lax
from jax.experimental import pallas as pl
from jax.experimental.pallas import tpu as pltpu

F32 = jnp.float32
BF16 = jnp.bfloat16

D_MODEL = 1024
BATCH = 4
SEQ = 4096
DEPTH = 4
DEC_BATCH = 32
DEC_SEQ = 4
HEAD_DIM = 64
N_SLOTS = 8
DIL_WINDOWS = (128, 512, 2048)
DIL_RATES = (1, 4, 16)
N_GROUPS_A = 3
A_WIDTH = N_SLOTS * HEAD_DIM
QKV_WIDTH = 3 * N_GROUPS_A * A_WIDTH
KV_WIDTH = 2 * N_GROUPS_A * A_WIDTH
BAND = 128
CHUNK = 128
D_V = 3072
N_GROUPS_B = 8
GROUP_B = D_V // N_GROUPS_B
D_FF = 2816
RMS_EPS = 1e-6
LN_EPS = 1e-5
NEG = -1e30
LOG2_E = 1.4426950408889634

VMEM_LIMIT_BYTES = 56 * 1024 * 1024
LANES = 128
BF16_ROWS = 16
FREE_STRIDE = 4
STACK_COLS = 256
TILE = 512
TILES_PER_SEQ = SEQ // TILE
MM_COLS = 256
FF_COLS = MM_COLS
UV_COLS = 768
Q_ROWS = 256
NEW_PAD = 16


def _params(n_axes):
    return pltpu.CompilerParams(dimension_semantics=("arbitrary",) * n_axes,
                                vmem_limit_bytes=VMEM_LIMIT_BYTES)


def _resident(shape, index_map):
    return pl.BlockSpec(shape, index_map, pipeline_mode=pl.Buffered(1))


def _rms(x, g):
    return x * lax.rsqrt(jnp.mean(x * x, axis=-1, keepdims=True) + RMS_EPS) * g


def _dot(a, b):
    return jnp.dot(a, b, preferred_element_type=F32)


def _dot_nt(a, b):
    return lax.dot_general(a, b, (((1,), (1,)), ((), ())), preferred_element_type=F32)


def _cast_plan(src, layer, n_steps):
    rows, cols = src.shape[1:]
    rb = rows // n_steps
    if rows % n_steps or rb % BF16_ROWS:
        rb = LANES
    nb = rows // rb
    assert rows % rb == 0 and nb <= n_steps
    in_spec = pl.BlockSpec((None, rb, cols), lambda i: (layer, jnp.minimum(i, nb - 1), 0))
    out_spec = pl.BlockSpec((rb, cols), lambda i: (jnp.minimum(i, nb - 1), 0))
    return in_spec, out_spec, jax.ShapeDtypeStruct((rows, cols), BF16)


def _cast_rows(src_refs, dst_refs):
    for src, dst in zip(src_refs, dst_refs, strict=True):
        dst[...] = src[...].astype(BF16)


def _stack_plan(per_layer, n_steps):
    n_batch, *mid, width = per_layer[0].shape
    wb = min(width, STACK_COLS)
    per_batch = width // wb
    nb = n_batch * per_batch
    assert width % wb == 0 and nb <= n_steps

    def at(i):
        k = jnp.minimum(i, nb - 1)
        return k // per_batch, k % per_batch

    in_spec = pl.BlockSpec((None, *mid, wb), lambda i: (at(i)[0], 0, 0, 0, at(i)[1]))
    out_spec = pl.BlockSpec((len(per_layer), None, *mid, wb),
                            lambda i: (0, at(i)[0], 0, 0, 0, at(i)[1]))
    out_shape = jax.ShapeDtypeStruct((len(per_layer), n_batch, *mid, width), F32)
    return [in_spec] * len(per_layer), out_spec, out_shape


def _stack_layers(src_refs, dst_refs):
    per = len(src_refs) // max(len(dst_refs), 1)
    for k, dst in enumerate(dst_refs):
        for layer in range(per):
            dst[layer] = src_refs[k * per + layer][...]


def _merge(outs, lses):
    mx = jnp.maximum(jnp.maximum(lses[0], lses[1]), lses[2])
    e = [jnp.exp(l - mx) for l in lses]
    den = e[0] + e[1] + e[2]
    return (e[0] / den) * outs[0] + (e[1] / den) * outs[1] + (e[2] / den) * outs[2]


def _position_order(slab_ref, slab2_ref, src_ref, d):
    n = TILE // d
    n_cols = A_WIDTH // LANES
    two_pass = d > FREE_STRIDE
    per = d // FREE_STRIDE if two_pass else d
    m = TILE // FREE_STRIDE
    dst = slab2_ref if two_pass else slab_ref
    for k in range(d):
        r0, r1 = divmod(k, per)
        for cc in range(n_cols):
            dst[cc, pl.ds(r0 * m * two_pass + r1, n, stride=per), :] = (
                src_ref[k * n:(k + 1) * n, cc * LANES:(cc + 1) * LANES])
    if two_pass:
        for r0 in range(FREE_STRIDE):
            for cc in range(n_cols):
                slab_ref[cc, pl.ds(r0, m, stride=FREE_STRIDE), :] = slab2_ref[cc, r0 * m:(r0 + 1) * m, :]
    return [slab_ref[cc] for cc in range(n_cols)]


def _merged_attention(o0, o1, o2, l0, l1, l2, slab_ref):
    cols = [slice(cc * LANES, (cc + 1) * LANES) for cc in range(A_WIDTH // LANES)]
    outs = [[o0[:, c] for c in cols],
            _position_order(slab_ref.at[0], None, o1, DIL_RATES[1]),
            _position_order(slab_ref.at[1], slab_ref.at[4], o2, DIL_RATES[2])]
    lses = [[l0[:, c] for c in cols],
            _position_order(slab_ref.at[2], None, l1, DIL_RATES[1]),
            _position_order(slab_ref.at[3], slab_ref.at[5], l2, DIL_RATES[2])]
    return jnp.concatenate(
        [_merge([o[cc] for o in outs], [l[cc] for l in lses]) for cc in range(len(cols))], axis=1)


def _ffn_kernel(*refs, final, mixer, n_cast, n_stack_src, n_stack, side):
    x_ref, g_ref, win_ref, wout_ref, gf_ref = refs[:5]
    refs = refs[5:]
    if mixer:
        *attn_refs, wmix_ref = refs[:7]
        refs = refs[7:]
    cast_src, refs = refs[:n_cast], refs[n_cast:]
    stack_src, refs = refs[:n_stack_src], refs[n_stack_src:]
    if side:
        xs_ref, refs = refs[0], refs[1:]
    o_ref, cast_dst, refs = refs[0], refs[1:n_cast + 1], refs[n_cast + 1:]
    stack_dst, refs = refs[:n_stack], refs[n_stack:]
    if side:
        os_ref, refs = refs[0], refs[1:]
    a_ref = refs[0]
    _stack_layers(stack_src, stack_dst)

    def half_step(x):
        rows = x.shape[0]
        h = _rms(x, g_ref[...]).astype(BF16)
        for c in range(D_FF // FF_COLS):
            lo = c * FF_COLS
            gate = _dot(h, win_ref[:, lo:lo + FF_COLS])
            up = _dot(h, win_ref[:, D_FF + lo:D_FF + lo + FF_COLS])
            a_ref[:rows, lo:lo + FF_COLS] = (gate * jax.nn.sigmoid(gate) * up).astype(BF16)
        y = x + 0.5 * _dot(a_ref[:rows], wout_ref[...])
        return _rms(y, gf_ref[...]) if final else y

    _cast_rows(cast_src, cast_dst)
    x = x_ref[...]
    if mixer:
        x = x + _dot(_merged_attention(*attn_refs, refs[1]).astype(BF16), wmix_ref[...])
    o_ref[...] = half_step(x)
    if side:
        @pl.when(pl.program_id(0) == pl.num_programs(0) - 1)
        def _():
            os_ref[...] = half_step(xs_ref[...])


def _ffn(x, layer, g, w_in, w_out, g_final, *, tm, final=False, mixer=None, casts=(), side=None,
         stacks=()):
    m = x.shape[0]
    steps = m // tm
    plans = [_cast_plan(src, lyr, steps) for src, lyr in casts]
    stack_plans = [_stack_plan(per_layer, steps) for per_layer in stacks]
    stack_in = [spec for p in stack_plans for spec in p[0]]
    stack_args = [a for per_layer in stacks for a in per_layer]
    in_specs = [
        pl.BlockSpec((tm, D_MODEL), lambda i: (i, 0)),
        pl.BlockSpec((None, 1, D_MODEL), lambda i: (layer, 0, 0)),
        _resident((D_MODEL, 2 * D_FF), lambda i: (0, 0)),
        _resident((D_FF, D_MODEL), lambda i: (0, 0)),
        pl.BlockSpec((1, D_MODEL), lambda i: (0, 0)),
    ]
    args = [x, g, w_in, w_out, g_final]
    scratch = [pltpu.VMEM((tm, D_FF), BF16)]
    if mixer is not None:
        outs, lses, w_mix = mixer
        in_specs += [pl.BlockSpec((tm, A_WIDTH), lambda i: (i, 0))] * 6
        in_specs.append(_resident((A_WIDTH, D_MODEL), lambda i: (0, 0)))
        args += [*outs, *lses, w_mix]
        scratch.append(pltpu.VMEM((6, A_WIDTH // LANES, TILE, LANES), F32))
    side_args, side_specs, side_shapes = [], [], []
    if side is not None:
        side_args = [side]
        side_specs = [pl.BlockSpec(side.shape, lambda i: (0, 0))]
        side_shapes = [jax.ShapeDtypeStruct(side.shape, F32)]
    res = pl.pallas_call(
        functools.partial(_ffn_kernel, final=final, mixer=mixer is not None, n_cast=len(plans),
                          n_stack_src=len(stack_in), n_stack=len(stack_plans),
                          side=side is not None),
        grid=(steps,),
        in_specs=in_specs + [p[0] for p in plans] + stack_in + side_specs,
        out_specs=([pl.BlockSpec((tm, D_MODEL), lambda i: (i, 0))] + [p[1] for p in plans]
                   + [p[1] for p in stack_plans] + side_specs),
        out_shape=([jax.ShapeDtypeStruct((m, D_MODEL), F32)] + [p[2] for p in plans]
                   + [p[2] for p in stack_plans] + side_shapes),
        scratch_shapes=scratch,
        compiler_params=_params(1),
        name="ffn",
    )(*args, *[src for src, _ in casts], *stack_args, *side_args)
    n = 1 + len(plans)
    ns = n + len(stack_plans)
    return res[0], res[1:n], (res[ns] if side is not None else None), res[n:ns]


def _qkv_sample_kernel(x_ref, g_ref, w_ref, qkv_ref, kv_ref):
    h = _rms(x_ref[...], g_ref[...]).astype(BF16)
    for c in range(QKV_WIDTH // A_WIDTH):
        lo = c * A_WIDTH
        part, grp = divmod(c, N_GROUPS_A)
        y = _dot(h, w_ref[:, lo:lo + A_WIDTH])
        if part == 0:
            qkv_ref[:, lo:lo + A_WIDTH] = (y * (HEAD_DIM ** -0.5)).astype(BF16)
        else:
            qkv_ref[:, lo:lo + A_WIDTH] = y.astype(BF16)
            dst = (2 * grp + part - 1) * A_WIDTH
            kv_ref[:, dst:dst + A_WIDTH] = y


def _qkv_sample(x, layer, g, w):
    m = x.shape[0]
    return pl.pallas_call(
        _qkv_sample_kernel,
        grid=(1,),
        in_specs=[
            pl.BlockSpec((m, D_MODEL), lambda i: (0, 0)),
            pl.BlockSpec((None, 1, D_MODEL), lambda i: (layer, 0, 0)),
            pl.BlockSpec((D_MODEL, QKV_WIDTH), lambda i: (0, 0)),
        ],
        out_specs=[
            pl.BlockSpec((m, QKV_WIDTH), lambda i: (0, 0)),
            pl.BlockSpec((m, KV_WIDTH), lambda i: (0, 0)),
        ],
        out_shape=[jax.ShapeDtypeStruct((m, QKV_WIDTH), BF16),
                   jax.ShapeDtypeStruct((m, KV_WIDTH), F32)],
        compiler_params=_params(1),
        name="qkv_sample",
    )(x, g, w)


def _regroup_rows(slab_ref, slab2_ref, y, dst_ref, col0, d):
    n = TILE // d
    cols = [slice(cc * LANES, (cc + 1) * LANES) for cc in range(A_WIDTH // LANES)]
    for cc, c in enumerate(cols):
        slab_ref[cc] = y[:, c]
    if d > FREE_STRIDE:
        m = TILE // FREE_STRIDE
        for r0 in range(FREE_STRIDE):
            for cc in range(len(cols)):
                slab2_ref[cc, r0 * m:(r0 + 1) * m, :] = slab_ref[cc, pl.ds(r0, m, stride=FREE_STRIDE), :]
        src, per, step = slab2_ref, d // FREE_STRIDE, m
    else:
        src, per, step = slab_ref, d, 0
    for k in range(d):
        r0, r1 = divmod(k, per)
        for cc, c in enumerate(cols):
            dst_ref[k * n:(k + 1) * n, col0 + c.start:col0 + c.stop] = (
                src[cc, pl.ds(r0 * step + r1, n, stride=per), :].astype(BF16))


def _qkv_prompt_kernel(*refs, n_cast):
    x_ref, g_ref, w_ref = refs[:3]
    cast_src = refs[3:3 + n_cast]
    outs = refs[3 + n_cast:]
    dst, win = outs[:N_GROUPS_A], outs[N_GROUPS_A:2 * N_GROUPS_A]
    cast_dst = outs[2 * N_GROUPS_A:2 * N_GROUPS_A + n_cast]
    slab_ref = outs[2 * N_GROUPS_A + n_cast]
    _cast_rows(cast_src, cast_dst)
    h = _rms(x_ref[...], g_ref[...]).astype(BF16)
    for c in range(QKV_WIDTH // A_WIDTH):
        part, grp = divmod(c, N_GROUPS_A)
        y = _dot(h, w_ref[:, c * A_WIDTH:(c + 1) * A_WIDTH])
        if part == 0:
            y = y * (HEAD_DIM ** -0.5 * LOG2_E)
        col0 = part * A_WIDTH
        if grp == 0:
            dst[0][:, col0:col0 + A_WIDTH] = y.astype(BF16)
        else:
            _regroup_rows(slab_ref.at[(grp - 1) * 3 + part], slab_ref.at[6 + part], y, dst[grp],
                          col0, DIL_RATES[grp])
        if part > 0:
            keep = min(DIL_WINDOWS[grp], TILE)
            win[grp][part - 1] = y[TILE - keep:].T.reshape(N_SLOTS, HEAD_DIM, keep)


def _qkv_prompt(x, layer, g, w, casts=()):
    m = x.shape[0]
    steps = m // TILE
    plans = [_cast_plan(src, lyr, steps) for src, lyr in casts]
    row_spec = pl.BlockSpec((TILE, 3 * A_WIDTH), lambda i: (i, 0))
    win_specs, win_shapes = [], []
    for grp in range(N_GROUPS_A):
        keep = min(DIL_WINDOWS[grp], TILE)
        first_tile = TILES_PER_SEQ - max(DIL_WINDOWS[grp] // TILE, 1)
        win_specs.append(pl.BlockSpec(
            (None, 2, N_SLOTS, HEAD_DIM, keep),
            lambda i, ft=first_tile: (i // TILES_PER_SEQ, 0, 0, 0,
                                      jnp.maximum(i % TILES_PER_SEQ - ft, 0))))
        win_shapes.append(jax.ShapeDtypeStruct(
            (BATCH, 2, N_SLOTS, HEAD_DIM, DIL_WINDOWS[grp]), F32))
    res = pl.pallas_call(
        functools.partial(_qkv_prompt_kernel, n_cast=len(plans)),
        grid=(steps,),
        in_specs=[
            pl.BlockSpec((TILE, D_MODEL), lambda i: (i, 0)),
            pl.BlockSpec((None, 1, D_MODEL), lambda i: (layer, 0, 0)),
            _resident((D_MODEL, QKV_WIDTH), lambda i: (0, 0)),
        ] + [p[0] for p in plans],
        out_specs=[row_spec] * N_GROUPS_A + win_specs + [p[1] for p in plans],
        out_shape=([jax.ShapeDtypeStruct((m, 3 * A_WIDTH), BF16)] * N_GROUPS_A + win_shapes
                   + [p[2] for p in plans]),
        scratch_shapes=[pltpu.VMEM((9, A_WIDTH // LANES, TILE, LANES), F32)],
        compiler_params=_params(1),
        name="qkv_prompt",
    )(x, g, w, *[src for src, _ in casts])
    return res[:N_GROUPS_A], res[N_GROUPS_A:2 * N_GROUPS_A], res[2 * N_GROUPS_A:]


def _band_kernel(mask_ref, q_ref, kp_ref, ko_ref, vp_ref, vo_ref, o_ref, l_ref, s_ref, p_ref, m_ref):
    n = q_ref.shape[1]
    n_sub, n_pair = Q_ROWS // BAND, A_WIDTH // LANES
    bias = mask_ref[0]
    bias_first = mask_ref[(pl.program_id(2) == 0).astype(jnp.int32)]
    low = lax.broadcasted_iota(jnp.int32, (1, LANES), 1) < HEAD_DIM
    q_all = q_ref[...].reshape(Q_ROWS, A_WIDTH)
    ko_all = ko_ref[...].reshape(Q_ROWS, A_WIDTH)
    vo_all = vo_ref[...].reshape(Q_ROWS, A_WIDTH)
    kp_all = kp_ref[...].reshape(BAND, A_WIDTH)
    vp_all = vp_ref[...].reshape(BAND, A_WIDTH)

    def keys(prev, own, j, cs):
        if j == 0:
            return jnp.concatenate([prev[:, cs], own[:BAND, cs]], axis=0)
        return own[:, cs]

    for pr in range(n_pair):
        cs = slice(pr * LANES, (pr + 1) * LANES)
        for j in range(n_sub):
            q = q_all[j * BAND:(j + 1) * BAND, cs]
            zero = jnp.zeros_like(q)
            q_ab = jnp.concatenate([jnp.where(low, q, zero), jnp.where(low, zero, q)], axis=0)
            s = _dot_nt(q_ab, keys(kp_all, ko_all, j, cs))
            s_ref[pr * n_sub + j] = s + (bias_first if j == 0 else bias)

    for u in range(n_pair * n_sub):
        mx = jnp.max(s_ref[u], axis=-1, keepdims=True)
        p_ref[u] = jnp.exp2(s_ref[u] - mx).astype(BF16)
        m_ref[u] = jnp.where(low, mx[:BAND], mx[BAND:])

    one = jnp.ones((2 * BAND, LANES), BF16)
    for pr in range(n_pair):
        cs = slice(pr * LANES, (pr + 1) * LANES)
        for j in range(n_sub):
            u = pr * n_sub + j
            vv = keys(vp_all, vo_all, j, cs)
            oa = _dot(p_ref[u, :BAND], jnp.where(low, vv, one))
            ob = _dot(p_ref[u, BAND:], jnp.where(low, one, vv))
            den = pltpu.roll(jnp.where(low, ob, oa), HEAD_DIM, axis=1)
            lse = (m_ref[u] + jnp.log2(den)) * (1.0 / LOG2_E)
            tiles = slice(j * (BAND // n), (j + 1) * (BAND // n))
            o_ref[tiles, :, cs] = (jnp.where(low, oa, ob) / den).reshape(BAND // n, n, LANES)
            l_ref[tiles, :, cs] = lse.reshape(BAND // n, n, LANES)


def _band_attention(qkv, grp):
    d = DIL_RATES[grp]
    n = min(TILE // d, BAND)
    pieces = BATCH * SEQ // (d * n)
    nb = SEQ // (d * Q_ROWS)
    units = (Q_ROWS // BAND) * (A_WIDTH // LANES)
    view = qkv.reshape(pieces, d, n, 3 * A_WIDTH)

    def own(part):
        return pl.BlockSpec((Q_ROWS // n, None, n, A_WIDTH), lambda b, r, i: (b * nb + i, r, 0, part))

    def prev(part):
        return pl.BlockSpec((BAND // n, None, n, A_WIDTH),
                            lambda b, r, i: (jnp.maximum((b * nb + i) * (Q_ROWS // BAND) - 1, 0),
                                             r, 0, part))

    qi = lax.broadcasted_iota(jnp.int32, (2 * BAND, 2 * BAND), 0) % BAND
    ki = lax.broadcasted_iota(jnp.int32, (2 * BAND, 2 * BAND), 1)
    band = (ki >= qi) & (ki <= qi + BAND)
    masks = jnp.where(jnp.stack([band, band & (ki >= BAND)]), 0.0, NEG).astype(F32)

    out_spec = pl.BlockSpec((Q_ROWS // n, None, n, A_WIDTH), lambda b, r, i: (b * nb + i, r, 0, 0))
    out_shape = jax.ShapeDtypeStruct((pieces, d, n, A_WIDTH), F32)
    o, lse = pl.pallas_call(
        _band_kernel,
        grid=(BATCH, d, nb),
        in_specs=[pl.BlockSpec(masks.shape, lambda b, r, i: (0, 0, 0)),
                  own(0), prev(1), own(1), prev(2), own(2)],
        out_specs=[out_spec, out_spec],
        out_shape=[out_shape, out_shape],
        scratch_shapes=[pltpu.VMEM((units, 2 * BAND, 2 * BAND), F32),
                        pltpu.VMEM((units, 2 * BAND, 2 * BAND), BF16),
                        pltpu.VMEM((units, BAND, LANES), F32)],
        compiler_params=_params(3),
        name=f"band_attention_d{d}",
    )(masks, view, view, view, view, view)
    return o.reshape(BATCH * SEQ, A_WIDTH), lse.reshape(BATCH * SEQ, A_WIDTH)


def _sample_attn_kernel(qkv_ref, c0_ref, c1_ref, c2_ref, y_ref):
    n_rows = DEC_SEQ * N_SLOTS
    qkv = qkv_ref[...].astype(F32)
    row_h = lax.broadcasted_iota(jnp.int32, (n_rows, A_WIDTH), 0) % N_SLOTS
    col_h = lax.broadcasted_iota(jnp.int32, (n_rows, A_WIDTH), 1) // HEAD_DIM
    own_head = row_h == col_h
    pad = jnp.zeros((NEW_PAD - DEC_SEQ, A_WIDTH), F32)

    def reach(n_keys, offset, d):
        t = lax.broadcasted_iota(jnp.int32, (n_rows, n_keys), 0) // N_SLOTS
        back = t - lax.broadcasted_iota(jnp.int32, (n_rows, n_keys), 1) - offset
        return (back >= 0) & (back <= BAND * d) & ((back & (d - 1)) == 0)

    outs, lses = [], []
    for grp, c_ref in enumerate((c0_ref, c1_ref, c2_ref)):
        d, window = DIL_RATES[grp], DIL_WINDOWS[grp]
        q = qkv[:, grp * A_WIDTH:(grp + 1) * A_WIDTH]
        k_new = qkv[:, (N_GROUPS_A + grp) * A_WIDTH:(N_GROUPS_A + grp + 1) * A_WIDTH]
        v_new = qkv[:, (2 * N_GROUPS_A + grp) * A_WIDTH:(2 * N_GROUPS_A + grp + 1) * A_WIDTH]
        q_rep = jnp.concatenate(
            [jnp.broadcast_to(q[t:t + 1], (N_SLOTS, A_WIDTH)) for t in range(DEC_SEQ)], axis=0)
        q_bd = jnp.where(own_head, q_rep, 0.0).astype(BF16)
        k_new = jnp.concatenate([k_new, pad], axis=0).astype(BF16)
        v_new = jnp.concatenate([v_new, pad], axis=0).astype(BF16)
        s_c = jnp.where(reach(window, -window, d), _dot(q_bd, c_ref[0].astype(BF16)), NEG)
        s_n = jnp.where(reach(NEW_PAD, 0, d), _dot_nt(q_bd, k_new), NEG)
        mx = jnp.maximum(jnp.max(s_c, axis=-1, keepdims=True), jnp.max(s_n, axis=-1, keepdims=True))
        p_c = jnp.exp(s_c - mx)
        p_n = jnp.exp(s_n - mx)
        den = jnp.sum(p_c, axis=-1, keepdims=True) + jnp.sum(p_n, axis=-1, keepdims=True)
        o = (_dot_nt(p_c.astype(BF16), c_ref[1].astype(BF16)) + _dot(p_n.astype(BF16), v_new)) / den
        lse = jnp.broadcast_to(mx + jnp.log(den), (n_rows, A_WIDTH))
        o = jnp.where(own_head, o, 0.0)
        lse = jnp.where(own_head, lse, 0.0)
        outs.append(jnp.concatenate(
            [jnp.sum(o[t * N_SLOTS:(t + 1) * N_SLOTS], axis=0, keepdims=True) for t in range(DEC_SEQ)],
            axis=0))
        lses.append(jnp.concatenate(
            [jnp.sum(lse[t * N_SLOTS:(t + 1) * N_SLOTS], axis=0, keepdims=True) for t in range(DEC_SEQ)],
            axis=0))
    y_ref[...] = _merge(outs, lses)


def _sample_attention(qkv, caches, li):
    views = [jnp.transpose(c, (0, 1, 3, 4, 5, 2)).reshape(-1, DEC_BATCH, 2, A_WIDTH, c.shape[2])
             for c in caches]
    return pl.pallas_call(
        _sample_attn_kernel,
        grid=(DEC_BATCH,),
        in_specs=[pl.BlockSpec((None, DEC_SEQ, QKV_WIDTH), lambda b: (b, 0, 0))] + [
            pl.BlockSpec((None, None, 2, A_WIDTH, w), lambda b: (li, b, 0, 0, 0)) for w in DIL_WINDOWS],
        out_specs=pl.BlockSpec((None, DEC_SEQ, A_WIDTH), lambda b: (b, 0, 0)),
        out_shape=jax.ShapeDtypeStruct((DEC_BATCH, DEC_SEQ, A_WIDTH), F32),
        compiler_params=_params(1),
        name="sample_attention",
    )(qkv.reshape(DEC_BATCH, DEC_SEQ, QKV_WIDTH), *views)


def _proj_kernel(x_ref, y_ref, w_ref, out_ref):
    out_ref[...] = x_ref[...] + _dot(y_ref[...].astype(BF16), w_ref[...])


def _proj(x, y, w):
    m = x.shape[0]
    return pl.pallas_call(
        _proj_kernel,
        grid=(1,),
        in_specs=[pl.BlockSpec((m, D_MODEL), lambda i: (0, 0)),
                  pl.BlockSpec((m, A_WIDTH), lambda i: (0, 0)),
                  pl.BlockSpec((A_WIDTH, D_MODEL), lambda i: (0, 0))],
        out_specs=pl.BlockSpec((m, D_MODEL), lambda i: (0, 0)),
        out_shape=jax.ShapeDtypeStruct((m, D_MODEL), F32),
        compiler_params=_params(1),
        name="proj",
    )(x, y, w)


def _gelu(x):
    a0 = -2.0 * (2.0 / jnp.pi) ** 0.5 * LOG2_E
    return x / (1.0 + jnp.exp2(x * (a0 + (a0 * 0.044715) * (x * x))))


def _gmlp_kernel(*refs, tm, sample, n_cast):
    x_ref, g_ref, wuv_ref, lng_ref, lnb_ref, ws_ref, bs_ref, wo_ref = refs[:8]
    cast_src = refs[8:8 + n_cast]
    refs = refs[8 + n_cast:]
    out_ref = refs[0]
    if sample:
        v_ref = refs[1]
    refs = refs[2:] if sample else refs[1:]
    cast_dst = refs[:n_cast]
    zv_ref, vn_ref, um_ref = refs[n_cast:]
    _cast_rows(cast_src, cast_dst)
    x = x_ref[...]
    h = _rms(x, g_ref[...]).astype(BF16)
    n_uv = D_V // UV_COLS

    tot = jnp.zeros((tm, 1), F32)
    for c in range(n_uv):
        lo = c * UV_COLS
        z = _gelu(_dot(h, wuv_ref[:, D_V + lo:D_V + lo + UV_COLS]))
        zv_ref[:, lo:lo + UV_COLS] = z
        tot = tot + jnp.sum(z, axis=-1, keepdims=True)
    mu = tot / D_V
    sq = jnp.zeros((tm, 1), F32)
    for c in range(n_uv):
        zc = zv_ref[:, c * UV_COLS:(c + 1) * UV_COLS] - mu
        sq = sq + jnp.sum(zc * zc, axis=-1, keepdims=True)
    rstd = lax.rsqrt(sq / D_V + LN_EPS)
    for c in range(n_uv):
        cols = slice(c * UV_COLS, (c + 1) * UV_COLS)
        vn = (zv_ref[:, cols] - mu) * rstd * lng_ref[:, cols] + lnb_ref[:, cols]
        vn_ref[:, cols] = vn.astype(BF16)
        if sample:
            v_ref[:, cols] = vn

    ri = lax.broadcasted_iota(jnp.int32, (CHUNK, CHUNK), 0)
    ci = lax.broadcasted_iota(jnp.int32, (CHUNK, CHUNK), 1)
    causal = ci <= ri
    if sample:
        causal = causal & ((ri // DEC_SEQ) == (ci // DEC_SEQ))
    groups_per_mm = UV_COLS // GROUP_B
    for c in range(n_uv):
        u = _gelu(_dot(h, wuv_ref[:, c * UV_COLS:(c + 1) * UV_COLS]))
        for gl in range(groups_per_mm):
            grp = c * groups_per_mm + gl
            w = jnp.where(causal, ws_ref[grp], 0.0).astype(BF16)
            bias = bs_ref[:, grp:grp + 1]
            cols = slice(grp * GROUP_B, (grp + 1) * GROUP_B)
            for n in range(tm // CHUNK):
                rows = slice(n * CHUNK, (n + 1) * CHUNK)
                mixed = _dot(w, vn_ref[rows, cols]) + bias
                um_ref[rows, cols] = (u[rows, gl * GROUP_B:(gl + 1) * GROUP_B] * mixed).astype(BF16)
    out_ref[...] = x + _dot(um_ref[...], wo_ref[...])


def _gmlp(x, layer, g, w_uv, ln_g, ln_b, w_s, b_s, w_out, *, tm, sample, casts=()):
    m = x.shape[0]
    li = layer // 2
    steps = m // tm
    plans = [_cast_plan(src, lyr, steps) for src, lyr in casts]
    out_specs = [pl.BlockSpec((tm, D_MODEL), lambda i: (i, 0))]
    out_shape = [jax.ShapeDtypeStruct((m, D_MODEL), F32)]
    if sample:
        out_specs.append(pl.BlockSpec((tm, D_V), lambda i: (i, 0)))
        out_shape.append(jax.ShapeDtypeStruct((m, D_V), F32))
    n_main = len(out_specs)
    res = pl.pallas_call(
        functools.partial(_gmlp_kernel, tm=tm, sample=sample, n_cast=len(plans)),
        grid=(steps,),
        in_specs=[
            pl.BlockSpec((tm, D_MODEL), lambda i: (i, 0)),
            pl.BlockSpec((None, 1, D_MODEL), lambda i: (layer, 0, 0)),
            _resident((D_MODEL, 2 * D_V), lambda i: (0, 0)),
            pl.BlockSpec((None, 1, D_V), lambda i: (li, 0, 0)),
            pl.BlockSpec((None, 1, D_V), lambda i: (li, 0, 0)),
            pl.BlockSpec((None, N_GROUPS_B, CHUNK, CHUNK), lambda i: (li, 0, 0, 0)),
            pl.BlockSpec((None, CHUNK, N_GROUPS_B), lambda i: (li, 0, 0)),
            _resident((D_V, D_MODEL), lambda i: (0, 0)),
        ] + [p[0] for p in plans],
        out_specs=out_specs + [p[1] for p in plans],
        out_shape=out_shape + [p[2] for p in plans],
        scratch_shapes=[pltpu.VMEM((tm, D_V), F32), pltpu.VMEM((tm, D_V), BF16),
                        pltpu.VMEM((tm, D_V), BF16)],
        compiler_params=_params(1),
        name="gmlp",
    )(x, g, w_uv, ln_g, ln_b, w_s, b_s, w_out, *[src for src, _ in casts])
    return res[:n_main], res[n_main:]


def kernel(x_prompt, x_sample, cache_kv_w128, cache_kv_w512, cache_kv_w2048, norm_ffn1, w_ffn1_in,
           w_ffn1_out, norm_mix, norm_ffn2, w_ffn2_in, w_ffn2_out, w_qkv_a, w_out_a, w_uv_b,
           ln_v_gain, ln_v_bias, w_spatial, b_spatial, w_out_b, norm_final):
    caches = (cache_kv_w128, cache_kv_w512, cache_kv_w2048)
    mp, ms = BATCH * SEQ, DEC_BATCH * DEC_SEQ
    xp = x_prompt.reshape(mp, D_MODEL)
    xs = x_sample.reshape(ms, D_MODEL)

    g1 = norm_ffn1.reshape(DEPTH, 1, D_MODEL)
    gm = norm_mix.reshape(DEPTH, 1, D_MODEL)
    g2 = norm_ffn2.reshape(DEPTH, 1, D_MODEL)
    gf = norm_final.reshape(1, D_MODEL)
    lng = ln_v_gain.reshape(-1, 1, D_V)
    lnb = ln_v_bias.reshape(-1, 1, D_V)
    reps = CHUNK // DEC_SEQ
    ws_p = w_spatial
    bs_p = jnp.swapaxes(b_spatial, 1, 2)
    ws_s = jnp.tile(w_spatial[:, :, :DEC_SEQ, :DEC_SEQ], (1, 1, reps, reps))
    bs_s = jnp.swapaxes(jnp.tile(b_spatial[:, :, :DEC_SEQ], (1, 1, reps)), 1, 2)

    f32_weights = {"ffn1_in": w_ffn1_in, "ffn1_out": w_ffn1_out, "ffn2_in": w_ffn2_in,
                   "ffn2_out": w_ffn2_out, "qkv": w_qkv_a, "out_a": w_out_a, "uv": w_uv_b,
                   "out_b": w_out_b}
    bf16_weights = {("ffn1_in", 0): w_ffn1_in[0].astype(BF16),
                    ("ffn1_out", 0): w_ffn1_out[0].astype(BF16)}

    def jobs(*keys):
        return keys, tuple((f32_weights[name], idx) for name, idx in keys)

    def done(keys, converted):
        bf16_weights.update(zip(keys, converted, strict=True))

    def w(name, idx):
        return bf16_weights[(name, idx)]

    win_layers = [[] for _ in range(N_GROUPS_A)]
    windows, kv_s, v_rows = None, [], []
    for i in range(DEPTH):
        li = i // 2
        attention = i % 2 == 0
        last = i == DEPTH - 1

        if not attention:
            keys, casts = jobs(("uv", li), ("out_b", li), ("ffn2_in", i), ("ffn2_out", i))
        elif i == 0:
            keys, casts = jobs(("qkv", li), ("out_a", li), ("ffn2_in", i), ("ffn2_out", i))
        else:
            keys, casts = jobs(("ffn2_in", i), ("ffn2_out", i))
        xp, converted, xs, _ = _ffn(xp, i, g1, w("ffn1_in", i), w("ffn1_out", i), gf, tm=TILE,
                                    casts=casts, side=xs)
        done(keys, converted)

        keys, casts = jobs() if last else jobs(("ffn1_in", i + 1), ("ffn1_out", i + 1))
        if attention:
            qkv_g, wins, converted = _qkv_prompt(xp, i, gm, w("qkv", li), casts)
            done(keys, converted)
            for grp in range(N_GROUPS_A):
                win_layers[grp].append(wins[grp])
            parts = [_band_attention(qkv_g[grp], grp) for grp in range(N_GROUPS_A)]
            mixer = ([p[0] for p in parts], [p[1] for p in parts], w("out_a", li))
            qkv, kv = _qkv_sample(xs, i, gm, w("qkv", li))
            kv_s.append(kv)
            y = _sample_attention(qkv, caches, li)
            xs = _proj(xs, y.reshape(ms, A_WIDTH), w("out_a", li))
            xp, _, xs, _ = _ffn(xp, i, g2, w("ffn2_in", i), w("ffn2_out", i), gf, tm=TILE,
                                final=last, mixer=mixer, side=xs)
        else:
            (xp,), converted = _gmlp(xp, i, gm, w("uv", li), lng, lnb, ws_p, bs_p, w("out_b", li),
                                     tm=TILE, sample=False, casts=casts)
            done(keys, converted)
            (xs, v), _ = _gmlp(xs, i, gm, w("uv", li), lng, lnb, ws_s, bs_s, w("out_b", li),
                               tm=ms, sample=True)
            v_rows.append(v)
            keys, casts = jobs() if last else jobs(("qkv", li + 1), ("out_a", li + 1))
            stacks = tuple(tuple(layers) for layers in win_layers) if last else ()
            xp, converted, xs, stacked = _ffn(xp, i, g2, w("ffn2_in", i), w("ffn2_out", i), gf,
                                              tm=TILE, final=last, casts=casts, side=xs,
                                              stacks=stacks)
            done(keys, converted)
            if last:
                windows = stacked

    def prompt_window(grp):
        return jnp.transpose(windows[grp], (0, 1, 5, 2, 3, 4))

    def sample_rows(grp):
        return jnp.stack([kv.reshape(DEC_BATCH, DEC_SEQ, N_GROUPS_A, 2, N_SLOTS, HEAD_DIM)[:, :, grp]
                          for kv in kv_s])

    return (xp.reshape(BATCH, SEQ, D_MODEL), xs.reshape(DEC_BATCH, DEC_SEQ, D_MODEL),
            prompt_window(0), prompt_window(1), prompt_window(2),
            sample_rows(0), sample_rows(1), sample_rows(2),
            jnp.stack(v_rows).reshape(len(v_rows), DEC_BATCH, DEC_SEQ, D_V))
```

```python
import functools

import jax
import jax.numpy as jnp
from jax import lax
from jax.experimental import pallas as pl
from jax.experimental.pallas import tpu as pltpu

F32 = jnp.float32
BF16 = jnp.bfloat16

D_MODEL = 1024
BATCH = 4
SEQ = 4096
DEPTH = 4
DEC_BATCH = 32
DEC_SEQ = 4
HEAD_DIM = 64
N_SLOTS = 8
DIL_WINDOWS = (128, 512, 2048)
DIL_RATES = (1, 4, 16)
N_GROUPS_A = 3
A_WIDTH = N_SLOTS * HEAD_DIM
QKV_WIDTH = 3 * N_GROUPS_A * A_WIDTH
KV_WIDTH = 2 * N_GROUPS_A * A_WIDTH
BAND = 128
CHUNK = 128
D_V = 3072
N_GROUPS_B = 8
GROUP_B = D_V // N_GROUPS_B
D_FF = 2816
RMS_EPS = 1e-6
LN_EPS = 1e-5
NEG = -1e30
LOG2_E = 1.4426950408889634

VMEM_LIMIT_BYTES = 56 * 1024 * 1024
LANES = 128
BF16_ROWS = 16
FREE_STRIDE = 4
TILE = 512
TILES_PER_SEQ = SEQ // TILE
MM_COLS = 256
FF_COLS = MM_COLS
UV_COLS = 768
Q_ROWS = 256
NEW_PAD = 16


def _params(n_axes):
    return pltpu.CompilerParams(dimension_semantics=("arbitrary",) * n_axes,
                                vmem_limit_bytes=VMEM_LIMIT_BYTES)


def _resident(shape, index_map):
    return pl.BlockSpec(shape, index_map, pipeline_mode=pl.Buffered(1))


def _rms(x, g):
    return x * lax.rsqrt(jnp.mean(x * x, axis=-1, keepdims=True) + RMS_EPS) * g


def _dot(a, b):
    return jnp.dot(a, b, preferred_element_type=F32)


def _dot_nt(a, b):
    return lax.dot_general(a, b, (((1,), (1,)), ((), ())), preferred_element_type=F32)


def _cast_plan(src, layer, n_steps):
    rows, cols = src.shape[1:]
    rb = rows // n_steps
    if rows % n_steps or rb % BF16_ROWS:
        rb = LANES
    nb = rows // rb
    assert rows % rb == 0 and nb <= n_steps
    in_spec = pl.BlockSpec((None, rb, cols), lambda i: (layer, jnp.minimum(i, nb - 1), 0))
    out_spec = pl.BlockSpec((rb, cols), lambda i: (jnp.minimum(i, nb - 1), 0))
    return in_spec, out_spec, jax.ShapeDtypeStruct((rows, cols), BF16)


def _cast_rows(src_refs, dst_refs):
    for src, dst in zip(src_refs, dst_refs, strict=True):
        dst[...] = src[...].astype(BF16)


def _stack_copies(src_refs, dst_refs, sem):
    per = len(src_refs) // max(len(dst_refs), 1)
    return [pltpu.make_async_copy(src_refs[k * per + layer], dst.at[layer], sem.at[k * per + layer])
            for k, dst in enumerate(dst_refs) for layer in range(per)]


def _merge(outs, lses):
    mx = jnp.maximum(jnp.maximum(lses[0], lses[1]), lses[2])
    e = [jnp.exp(l - mx) for l in lses]
    den = e[0] + e[1] + e[2]
    return (e[0] / den) * outs[0] + (e[1] / den) * outs[1] + (e[2] / den) * outs[2]


def _position_order(slab_ref, slab2_ref, src_ref, d):
    n = TILE // d
    n_cols = A_WIDTH // LANES
    two_pass = d > FREE_STRIDE
    per = d // FREE_STRIDE if two_pass else d
    m = TILE // FREE_STRIDE
    dst = slab2_ref if two_pass else slab_ref
    for k in range(d):
        r0, r1 = divmod(k, per)
        for cc in range(n_cols):
            dst[cc, pl.ds(r0 * m * two_pass + r1, n, stride=per), :] = (
                src_ref[k * n:(k + 1) * n, cc * LANES:(cc + 1) * LANES])
    if two_pass:
        for r0 in range(FREE_STRIDE):
            for cc in range(n_cols):
                slab_ref[cc, pl.ds(r0, m, stride=FREE_STRIDE), :] = slab2_ref[cc, r0 * m:(r0 + 1) * m, :]
    return [slab_ref[cc] for cc in range(n_cols)]


def _merged_attention(o0, o1, o2, l0, l1, l2, slab_ref):
    cols = [slice(cc * LANES, (cc + 1) * LANES) for cc in range(A_WIDTH // LANES)]
    outs = [[o0[:, c] for c in cols],
            _position_order(slab_ref.at[0], None, o1, DIL_RATES[1]),
            _position_order(slab_ref.at[1], slab_ref.at[4], o2, DIL_RATES[2])]
    lses = [[l0[:, c] for c in cols],
            _position_order(slab_ref.at[2], None, l1, DIL_RATES[1]),
            _position_order(slab_ref.at[3], slab_ref.at[5], l2, DIL_RATES[2])]
    return jnp.concatenate(
        [_merge([o[cc] for o in outs], [l[cc] for l in lses]) for cc in range(len(cols))], axis=1)


def _ffn_kernel(*refs, final, mixer, n_cast, n_stack_src, n_stack, side):
    x_ref, g_ref, win_ref, wout_ref, gf_ref = refs[:5]
    refs = refs[5:]
    if mixer:
        *attn_refs, wmix_ref = refs[:7]
        refs = refs[7:]
    cast_src, refs = refs[:n_cast], refs[n_cast:]
    stack_src, refs = refs[:n_stack_src], refs[n_stack_src:]
    if side:
        xs_ref, refs = refs[0], refs[1:]
    o_ref, cast_dst, refs = refs[0], refs[1:n_cast + 1], refs[n_cast + 1:]
    stack_dst, refs = refs[:n_stack], refs[n_stack:]
    if side:
        os_ref, refs = refs[0], refs[1:]
    a_ref = refs[0]
    last_step = pl.num_programs(0) - 1
    if stack_src:
        copies = _stack_copies(stack_src, stack_dst, refs[-1])

        @pl.when(pl.program_id(0) == 0)
        def _():
            for copy in copies:
                copy.start()

    def half_step(x):
        rows = x.shape[0]
        h = _rms(x, g_ref[...]).astype(BF16)
        for c in range(D_FF // FF_COLS):
            lo = c * FF_COLS
            gate = _dot(h, win_ref[:, lo:lo + FF_COLS])
            up = _dot(h, win_ref[:, D_FF + lo:D_FF + lo + FF_COLS])
            a_ref[:rows, lo:lo + FF_COLS] = (gate * jax.nn.sigmoid(gate) * up).astype(BF16)
        y = x + 0.5 * _dot(a_ref[:rows], wout_ref[...])
        return _rms(y, gf_ref[...]) if final else y

    _cast_rows(cast_src, cast_dst)
    x = x_ref[...]
    if mixer:
        x = x + _dot(_merged_attention(*attn_refs, refs[1]).astype(BF16), wmix_ref[...])
    o_ref[...] = half_step(x)
    if side:
        @pl.when(pl.program_id(0) == last_step)
        def _():
            os_ref[...] = half_step(xs_ref[...])
    if stack_src:
        @pl.when(pl.program_id(0) == last_step)
        def _():
            for copy in copies:
                copy.wait()


def _ffn(x, layer, g, w_in, w_out, g_final, *, tm, final=False, mixer=None, casts=(), side=None,
         stacks=()):
    m = x.shape[0]
    steps = m // tm
    plans = [_cast_plan(src, lyr, steps) for src, lyr in casts]
    stack_args = [a for per_layer in stacks for a in per_layer]
    stack_in = [pl.BlockSpec(memory_space=pl.ANY)] * len(stack_args)
    stack_out = [pl.BlockSpec(memory_space=pl.ANY)] * len(stacks)
    stack_shapes = [jax.ShapeDtypeStruct((len(per_layer), *per_layer[0].shape), F32)
                    for per_layer in stacks]
    in_specs = [
        pl.BlockSpec((tm, D_MODEL), lambda i: (i, 0)),
        pl.BlockSpec((None, 1, D_MODEL), lambda i: (layer, 0, 0)),
        _resident((D_MODEL, 2 * D_FF), lambda i: (0, 0)),
        _resident((D_FF, D_MODEL), lambda i: (0, 0)),
        pl.BlockSpec((1, D_MODEL), lambda i: (0, 0)),
    ]
    args = [x, g, w_in, w_out, g_final]
    scratch = [pltpu.VMEM((tm, D_FF), BF16)]
    if mixer is not None:
        outs, lses, w_mix = mixer
        in_specs += [pl.BlockSpec((tm, A_WIDTH), lambda i: (i, 0))] * 6
        in_specs.append(_resident((A_WIDTH, D_MODEL), lambda i: (0, 0)))
        args += [*outs, *lses, w_mix]
        scratch.append(pltpu.VMEM((6, A_WIDTH // LANES, TILE, LANES), F32))
    side_args, side_specs, side_shapes = [], [], []
    if side is not None:
        side_args = [side]
        side_specs = [pl.BlockSpec(side.shape, lambda i: (0, 0))]
        side_shapes = [jax.ShapeDtypeStruct(side.shape, F32)]
    if stack_args:
        scratch.append(pltpu.SemaphoreType.DMA((len(stack_args),)))
    res = pl.pallas_call(
        functools.partial(_ffn_kernel, final=final, mixer=mixer is not None, n_cast=len(plans),
                          n_stack_src=len(stack_in), n_stack=len(stack_out),
                          side=side is not None),
        grid=(steps,),
        in_specs=in_specs + [p[0] for p in plans] + stack_in + side_specs,
        out_specs=([pl.BlockSpec((tm, D_MODEL), lambda i: (i, 0))] + [p[1] for p in plans]
                   + stack_out + side_specs),
        out_shape=([jax.ShapeDtypeStruct((m, D_MODEL), F32)] + [p[2] for p in plans]
                   + stack_shapes + side_shapes),
        scratch_shapes=scratch,
        compiler_params=_params(1),
        name="ffn",
    )(*args, *[src for src, _ in casts], *stack_args, *side_args)
    n = 1 + len(plans)
    ns = n + len(stack_out)
    return res[0], res[1:n], (res[ns] if side is not None else None), res[n:ns]


def _qkv_sample_kernel(x_ref, g_ref, w_ref, qkv_ref, kv_ref):
    h = _rms(x_ref[...], g_ref[...]).astype(BF16)
    for c in range(QKV_WIDTH // A_WIDTH):
        lo = c * A_WIDTH
        part, grp = divmod(c, N_GROUPS_A)
        y = _dot(h, w_ref[:, lo:lo + A_WIDTH])
        if part == 0:
            qkv_ref[:, lo:lo + A_WIDTH] = (y * (HEAD_DIM ** -0.5)).astype(BF16)
        else:
            qkv_ref[:, lo:lo + A_WIDTH] = y.astype(BF16)
            dst = (2 * grp + part - 1) * A_WIDTH
            kv_ref[:, dst:dst + A_WIDTH] = y


def _qkv_sample(x, layer, g, w):
    m = x.shape[0]
    return pl.pallas_call(
        _qkv_sample_kernel,
        grid=(1,),
        in_specs=[
            pl.BlockSpec((m, D_MODEL), lambda i: (0, 0)),
            pl.BlockSpec((None, 1, D_MODEL), lambda i: (layer, 0, 0)),
            pl.BlockSpec((D_MODEL, QKV_WIDTH), lambda i: (0, 0)),
        ],
        out_specs=[
            pl.BlockSpec((m, QKV_WIDTH), lambda i: (0, 0)),
            pl.BlockSpec((m, KV_WIDTH), lambda i: (0, 0)),
        ],
        out_shape=[jax.ShapeDtypeStruct((m, QKV_WIDTH), BF16),
                   jax.ShapeDtypeStruct((m, KV_WIDTH), F32)],
        compiler_params=_params(1),
        name="qkv_sample",
    )(x, g, w)


def _regroup_rows(slab_ref, slab2_ref, y, dst_ref, col0, d):
    n = TILE // d
    cols = [slice(cc * LANES, (cc + 1) * LANES) for cc in range(A_WIDTH // LANES)]
    for cc, c in enumerate(cols):
        slab_ref[cc] = y[:, c]
    if d > FREE_STRIDE:
        m = TILE // FREE_STRIDE
        for r0 in range(FREE_STRIDE):
            for cc in range(len(cols)):
                slab2_ref[cc, r0 * m:(r0 + 1) * m, :] = slab_ref[cc, pl.ds(r0, m, stride=FREE_STRIDE), :]
        src, per, step = slab2_ref, d // FREE_STRIDE, m
    else:
        src, per, step = slab_ref, d, 0
    for k in range(d):
        r0, r1 = divmod(k, per)
        for cc, c in enumerate(cols):
            dst_ref[k * n:(k + 1) * n, col0 + c.start:col0 + c.stop] = (
                src[cc, pl.ds(r0 * step + r1, n, stride=per), :].astype(BF16))


def _qkv_prompt_kernel(*refs, n_cast):
    x_ref, g_ref, w_ref = refs[:3]
    cast_src = refs[3:3 + n_cast]
    outs = refs[3 + n_cast:]
    dst, win = outs[:N_GROUPS_A], outs[N_GROUPS_A:2 * N_GROUPS_A]
    cast_dst = outs[2 * N_GROUPS_A:2 * N_GROUPS_A + n_cast]
    slab_ref = outs[2 * N_GROUPS_A + n_cast]
    _cast_rows(cast_src, cast_dst)
    h = _rms(x_ref[...], g_ref[...]).astype(BF16)
    for c in range(QKV_WIDTH // A_WIDTH):
        part, grp = divmod(c, N_GROUPS_A)
        y = _dot(h, w_ref[:, c * A_WIDTH:(c + 1) * A_WIDTH])
        if part == 0:
            y = y * (HEAD_DIM ** -0.5 * LOG2_E)
        col0 = part * A_WIDTH
        if grp == 0:
            dst[0][:, col0:col0 + A_WIDTH] = y.astype(BF16)
        else:
            _regroup_rows(slab_ref.at[(grp - 1) * 3 + part], slab_ref.at[6 + part], y, dst[grp],
                          col0, DIL_RATES[grp])
        if part > 0:
            keep = min(DIL_WINDOWS[grp], TILE)
            win[grp][part - 1] = y[TILE - keep:].T.reshape(N_SLOTS, HEAD_DIM, keep)


def _qkv_prompt(x, layer, g, w, casts=()):
    m = x.shape[0]
    steps = m // TILE
    plans = [_cast_plan(src, lyr, steps) for src, lyr in casts]
    row_spec = pl.BlockSpec((TILE, 3 * A_WIDTH), lambda i: (i, 0))
    win_specs, win_shapes = [], []
    for grp in range(N_GROUPS_A):
        keep = min(DIL_WINDOWS[grp], TILE)
        first_tile = TILES_PER_SEQ - max(DIL_WINDOWS[grp] // TILE, 1)
        win_specs.append(pl.BlockSpec(
            (None, 2, N_SLOTS, HEAD_DIM, keep),
            lambda i, ft=first_tile: (i // TILES_PER_SEQ, 0, 0, 0,
                                      jnp.maximum(i % TILES_PER_SEQ - ft, 0))))
        win_shapes.append(jax.ShapeDtypeStruct(
            (BATCH, 2, N_SLOTS, HEAD_DIM, DIL_WINDOWS[grp]), F32))
    res = pl.pallas_call(
        functools.partial(_qkv_prompt_kernel, n_cast=len(plans)),
        grid=(steps,),
        in_specs=[
            pl.BlockSpec((TILE, D_MODEL), lambda i: (i, 0)),
            pl.BlockSpec((None, 1, D_MODEL), lambda i: (layer, 0, 0)),
            _resident((D_MODEL, QKV_WIDTH), lambda i: (0, 0)),
        ] + [p[0] for p in plans],
        out_specs=[row_spec] * N_GROUPS_A + win_specs + [p[1] for p in plans],
        out_shape=([jax.ShapeDtypeStruct((m, 3 * A_WIDTH), BF16)] * N_GROUPS_A + win_shapes
                   + [p[2] for p in plans]),
        scratch_shapes=[pltpu.VMEM((9, A_WIDTH // LANES, TILE, LANES), F32)],
        compiler_params=_params(1),
        name="qkv_prompt",
    )(x, g, w, *[src for src, _ in casts])
    return res[:N_GROUPS_A], res[N_GROUPS_A:2 * N_GROUPS_A], res[2 * N_GROUPS_A:]


def _band_kernel(mask_ref, q_ref, kp_ref, ko_ref, vp_ref, vo_ref, o_ref, l_ref, s_ref, p_ref, m_ref):
    n = q_ref.shape[1]
    n_sub, n_pair = Q_ROWS // BAND, A_WIDTH // LANES
    bias = mask_ref[0]
    bias_first = mask_ref[(pl.program_id(2) == 0).astype(jnp.int32)]
    low = lax.broadcasted_iota(jnp.int32, (1, LANES), 1) < HEAD_DIM
    q_all = q_ref[...].reshape(Q_ROWS, A_WIDTH)
    ko_all = ko_ref[...].reshape(Q_ROWS, A_WIDTH)
    vo_all = vo_ref[...].reshape(Q_ROWS, A_WIDTH)
    kp_all = kp_ref[...].reshape(BAND, A_WIDTH)
    vp_all = vp_ref[...].reshape(BAND, A_WIDTH)

    def keys(prev, own, j, cs):
        if j == 0:
            return jnp.concatenate([prev[:, cs], own[:BAND, cs]], axis=0)
        return own[:, cs]

    for pr in range(n_pair):
        cs = slice(pr * LANES, (pr + 1) * LANES)
        for j in range(n_sub):
            q = q_all[j * BAND:(j + 1) * BAND, cs]
            zero = jnp.zeros_like(q)
            q_ab = jnp.concatenate([jnp.where(low, q, zero), jnp.where(low, zero, q)], axis=0)
            s = _dot_nt(q_ab, keys(kp_all, ko_all, j, cs))
            s_ref[pr * n_sub + j] = s + (bias_first if j == 0 else bias)

    for u in range(n_pair * n_sub):
        mx = jnp.max(s_ref[u], axis=-1, keepdims=True)
        p_ref[u] = jnp.exp2(s_ref[u] - mx).astype(BF16)
        m_ref[u] = jnp.where(low, mx[:BAND], mx[BAND:])

    one = jnp.ones((2 * BAND, LANES), BF16)
    for pr in range(n_pair):
        cs = slice(pr * LANES, (pr + 1) * LANES)
        for j in range(n_sub):
            u = pr * n_sub + j
            vv = keys(vp_all, vo_all, j, cs)
            oa = _dot(p_ref[u, :BAND], jnp.where(low, vv, one))
            ob = _dot(p_ref[u, BAND:], jnp.where(low, one, vv))
            den = pltpu.roll(jnp.where(low, ob, oa), HEAD_DIM, axis=1)
            lse = (m_ref[u] + jnp.log2(den)) * (1.0 / LOG2_E)
            tiles = slice(j * (BAND // n), (j + 1) * (BAND // n))
            o_ref[tiles, :, cs] = (jnp.where(low, oa, ob) / den).reshape(BAND // n, n, LANES)
            l_ref[tiles, :, cs] = lse.reshape(BAND // n, n, LANES)


def _band_attention(qkv, grp):
    d = DIL_RATES[grp]
    n = min(TILE // d, BAND)
    pieces = BATCH * SEQ // (d * n)
    nb = SEQ // (d * Q_ROWS)
    units = (Q_ROWS // BAND) * (A_WIDTH // LANES)
    view = qkv.reshape(pieces, d, n, 3 * A_WIDTH)

    def own(part):
        return pl.BlockSpec((Q_ROWS // n, None, n, A_WIDTH), lambda b, r, i: (b * nb + i, r, 0, part))

    def prev(part):
        return pl.BlockSpec((BAND // n, None, n, A_WIDTH),
                            lambda b, r, i: (jnp.maximum((b * nb + i) * (Q_ROWS // BAND) - 1, 0),
                                             r, 0, part))

    qi = lax.broadcasted_iota(jnp.int32, (2 * BAND, 2 * BAND), 0) % BAND
    ki = lax.broadcasted_iota(jnp.int32, (2 * BAND, 2 * BAND), 1)
    band = (ki >= qi) & (ki <= qi + BAND)
    masks = jnp.where(jnp.stack([band, band & (ki >= BAND)]), 0.0, NEG).astype(F32)

    out_spec = pl.BlockSpec((Q_ROWS // n, None, n, A_WIDTH), lambda b, r, i: (b * nb + i, r, 0, 0))
    out_shape = jax.ShapeDtypeStruct((pieces, d, n, A_WIDTH), F32)
    o, lse = pl.pallas_call(
        _band_kernel,
        grid=(BATCH, d, nb),
        in_specs=[pl.BlockSpec(masks.shape, lambda b, r, i: (0, 0, 0)),
                  own(0), prev(1), own(1), prev(2), own(2)],
        out_specs=[out_spec, out_spec],
        out_shape=[out_shape, out_shape],
        scratch_shapes=[pltpu.VMEM((units, 2 * BAND, 2 * BAND), F32),
                        pltpu.VMEM((units, 2 * BAND, 2 * BAND), BF16),
                        pltpu.VMEM((units, BAND, LANES), F32)],
        compiler_params=_params(3),
        name=f"band_attention_d{d}",
    )(masks, view, view, view, view, view)
    return o.reshape(BATCH * SEQ, A_WIDTH), lse.reshape(BATCH * SEQ, A_WIDTH)


def _sample_attn_kernel(qkv_ref, c0_ref, c1_ref, c2_ref, y_ref):
    n_rows = DEC_SEQ * N_SLOTS
    qkv = qkv_ref[...].astype(F32)
    row_h = lax.broadcasted_iota(jnp.int32, (n_rows, A_WIDTH), 0) % N_SLOTS
    col_h = lax.broadcasted_iota(jnp.int32, (n_rows, A_WIDTH), 1) // HEAD_DIM
    own_head = row_h == col_h
    pad = jnp.zeros((NEW_PAD - DEC_SEQ, A_WIDTH), F32)

    def reach(n_keys, offset, d):
        t = lax.broadcasted_iota(jnp.int32, (n_rows, n_keys), 0) // N_SLOTS
        back = t - lax.broadcasted_iota(jnp.int32, (n_rows, n_keys), 1) - offset
        return (back >= 0) & (back <= BAND * d) & ((back & (d - 1)) == 0)

    outs, lses = [], []
    for grp, c_ref in enumerate((c0_ref, c1_ref, c2_ref)):
        d, window = DIL_RATES[grp], DIL_WINDOWS[grp]
        q = qkv[:, grp * A_WIDTH:(grp + 1) * A_WIDTH]
        k_new = qkv[:, (N_GROUPS_A + grp) * A_WIDTH:(N_GROUPS_A + grp + 1) * A_WIDTH]
        v_new = qkv[:, (2 * N_GROUPS_A + grp) * A_WIDTH:(2 * N_GROUPS_A + grp + 1) * A_WIDTH]
        q_rep = jnp.concatenate(
            [jnp.broadcast_to(q[t:t + 1], (N_SLOTS, A_WIDTH)) for t in range(DEC_SEQ)], axis=0)
        q_bd = jnp.where(own_head, q_rep, 0.0).astype(BF16)
        k_new = jnp.concatenate([k_new, pad], axis=0).astype(BF16)
        v_new = jnp.concatenate([v_new, pad], axis=0).astype(BF16)
        s_c = jnp.where(reach(window, -window, d), _dot(q_bd, c_ref[0].astype(BF16)), NEG)
        s_n = jnp.where(reach(NEW_PAD, 0, d), _dot_nt(q_bd, k_new), NEG)
        mx = jnp.maximum(jnp.max(s_c, axis=-1, keepdims=True), jnp.max(s_n, axis=-1, keepdims=True))
        p_c = jnp.exp(s_c - mx)
        p_n = jnp.exp(s_n - mx)
        den = jnp.sum(p_c, axis=-1, keepdims=True) + jnp.sum(p_n, axis=-1, keepdims=True)
        o = (_dot_nt(p_c.astype(BF16), c_ref[1].astype(BF16)) + _dot(p_n.astype(BF16), v_new)) / den
        lse = jnp.broadcast_to(mx + jnp.log(den), (n_rows, A_WIDTH))
        o = jnp.where(own_head, o, 0.0)
        lse = jnp.where(own_head, lse, 0.0)
        outs.append(jnp.concatenate(
            [jnp.sum(o[t * N_SLOTS:(t + 1) * N_SLOTS], axis=0, keepdims=True) for t in range(DEC_SEQ)],
            axis=0))
        lses.append(jnp.concatenate(
            [jnp.sum(lse[t * N_SLOTS:(t + 1) * N_SLOTS], axis=0, keepdims=True) for t in range(DEC_SEQ)],
            axis=0))
    y_ref[...] = _merge(outs, lses)


def _sample_attention(qkv, caches, li):
    views = [jnp.transpose(c, (0, 1, 3, 4, 5, 2)).reshape(-1, DEC_BATCH, 2, A_WIDTH, c.shape[2])
             for c in caches]
    return pl.pallas_call(
        _sample_attn_kernel,
        grid=(DEC_BATCH,),
        in_specs=[pl.BlockSpec((None, DEC_SEQ, QKV_WIDTH), lambda b: (b, 0, 0))] + [
            pl.BlockSpec((None, None, 2, A_WIDTH, w), lambda b: (li, b, 0, 0, 0)) for w in DIL_WINDOWS],
        out_specs=pl.BlockSpec((None, DEC_SEQ, A_WIDTH), lambda b: (b, 0, 0)),
        out_shape=jax.ShapeDtypeStruct((DEC_BATCH, DEC_SEQ, A_WIDTH), F32),
        compiler_params=_params(1),
        name="sample_attention",
    )(qkv.reshape(DEC_BATCH, DEC_SEQ, QKV_WIDTH), *views)


def _proj_kernel(x_ref, y_ref, w_ref, out_ref):
    out_ref[...] = x_ref[...] + _dot(y_ref[...].astype(BF16), w_ref[...])


def _proj(x, y, w):
    m = x.shape[0]
    return pl.pallas_call(
        _proj_kernel,
        grid=(1,),
        in_specs=[pl.BlockSpec((m, D_MODEL), lambda i: (0, 0)),
                  pl.BlockSpec((m, A_WIDTH), lambda i: (0, 0)),
                  pl.BlockSpec((A_WIDTH, D_MODEL), lambda i: (0, 0))],
        out_specs=pl.BlockSpec((m, D_MODEL), lambda i: (0, 0)),
        out_shape=jax.ShapeDtypeStruct((m, D_MODEL), F32),
        compiler_params=_params(1),
        name="proj",
    )(x, y, w)


def _gelu(x):
    a0 = -2.0 * (2.0 / jnp.pi) ** 0.5 * LOG2_E
    return x / (1.0 + jnp.exp2(x * (a0 + (a0 * 0.044715) * (x * x))))


def _gmlp_kernel(*refs, tm, sample, n_cast):
    x_ref, g_ref, wuv_ref, lng_ref, lnb_ref, ws_ref, bs_ref, wo_ref = refs[:8]
    cast_src = refs[8:8 + n_cast]
    refs = refs[8 + n_cast:]
    out_ref = refs[0]
    if sample:
        v_ref = refs[1]
    refs = refs[2:] if sample else refs[1:]
    cast_dst = refs[:n_cast]
    zv_ref, vn_ref, um_ref = refs[n_cast:]
    _cast_rows(cast_src, cast_dst)
    x = x_ref[...]
    h = _rms(x, g_ref[...]).astype(BF16)
    n_uv = D_V // UV_COLS

    tot = jnp.zeros((tm, 1), F32)
    for c in range(n_uv):
        lo = c * UV_COLS
        z = _gelu(_dot(h, wuv_ref[:, D_V + lo:D_V + lo + UV_COLS]))
        zv_ref[:, lo:lo + UV_COLS] = z
        tot = tot + jnp.sum(z, axis=-1, keepdims=True)
    mu = tot / D_V
    sq = jnp.zeros((tm, 1), F32)
    for c in range(n_uv):
        zc = zv_ref[:, c * UV_COLS:(c + 1) * UV_COLS] - mu
        sq = sq + jnp.sum(zc * zc, axis=-1, keepdims=True)
    rstd = lax.rsqrt(sq / D_V + LN_EPS)
    for c in range(n_uv):
        cols = slice(c * UV_COLS, (c + 1) * UV_COLS)
        vn = (zv_ref[:, cols] - mu) * rstd * lng_ref[:, cols] + lnb_ref[:, cols]
        vn_ref[:, cols] = vn.astype(BF16)
        if sample:
            v_ref[:, cols] = vn

    ri = lax.broadcasted_iota(jnp.int32, (CHUNK, CHUNK), 0)
    ci = lax.broadcasted_iota(jnp.int32, (CHUNK, CHUNK), 1)
    causal = ci <= ri
    if sample:
        causal = causal & ((ri // DEC_SEQ) == (ci // DEC_SEQ))
    groups_per_mm = UV_COLS // GROUP_B
    for c in range(n_uv):
        u = _gelu(_dot(h, wuv_ref[:, c * UV_COLS:(c + 1) * UV_COLS]))
        for gl in range(groups_per_mm):
            grp = c * groups_per_mm + gl
            w = jnp.where(causal, ws_ref[grp], 0.0).astype(BF16)
            bias = bs_ref[:, grp:grp + 1]
            cols = slice(grp * GROUP_B, (grp + 1) * GROUP_B)
            for n in range(tm // CHUNK):
                rows = slice(n * CHUNK, (n + 1) * CHUNK)
                mixed = _dot(w, vn_ref[rows, cols]) + bias
                um_ref[rows, cols] = (u[rows, gl * GROUP_B:(gl + 1) * GROUP_B] * mixed).astype(BF16)
    out_ref[...] = x + _dot(um_ref[...], wo_ref[...])


def _gmlp(x, layer, g, w_uv, ln_g, ln_b, w_s, b_s, w_out, *, tm, sample, casts=()):
    m = x.shape[0]
    li = layer // 2
    steps = m // tm
    plans = [_cast_plan(src, lyr, steps) for src, lyr in casts]
    out_specs = [pl.BlockSpec((tm, D_MODEL), lambda i: (i, 0))]
    out_shape = [jax.ShapeDtypeStruct((m, D_MODEL), F32)]
    if sample:
        out_specs.append(pl.BlockSpec((tm, D_V), lambda i: (i, 0)))
        out_shape.append(jax.ShapeDtypeStruct((m, D_V), F32))
    n_main = len(out_specs)
    res = pl.pallas_call(
        functools.partial(_gmlp_kernel, tm=tm, sample=sample, n_cast=len(plans)),
        grid=(steps,),
        in_specs=[
            pl.BlockSpec((tm, D_MODEL), lambda i: (i, 0)),
            pl.BlockSpec((None, 1, D_MODEL), lambda i: (layer, 0, 0)),
            _resident((D_MODEL, 2 * D_V), lambda i: (0, 0)),
            pl.BlockSpec((None, 1, D_V), lambda i: (li, 0, 0)),
            pl.BlockSpec((None, 1, D_V), lambda i: (li, 0, 0)),
            pl.BlockSpec((None, N_GROUPS_B, CHUNK, CHUNK), lambda i: (li, 0, 0, 0)),
            pl.BlockSpec((None, CHUNK, N_GROUPS_B), lambda i: (li, 0, 0)),
            _resident((D_V, D_MODEL), lambda i: (0, 0)),
        ] + [p[0] for p in plans],
        out_specs=out_specs + [p[1] for p in plans],
        out_shape=out_shape + [p[2] for p in plans],
        scratch_shapes=[pltpu.VMEM((tm, D_V), F32), pltpu.VMEM((tm, D_V), BF16),
                        pltpu.VMEM((tm, D_V), BF16)],
        compiler_params=_params(1),
        name="gmlp",
    )(x, g, w_uv, ln_g, ln_b, w_s, b_s, w_out, *[src for src, _ in casts])
    return res[:n_main], res[n_main:]


def kernel(x_prompt, x_sample, cache_kv_w128, cache_kv_w512, cache_kv_w2048, norm_ffn1, w_ffn1_in,
           w_ffn1_out, norm_mix, norm_ffn2, w_ffn2_in, w_ffn2_out, w_qkv_a, w_out_a, w_uv_b,
           ln_v_gain, ln_v_bias, w_spatial, b_spatial, w_out_b, norm_final):
    caches = (cache_kv_w128, cache_kv_w512, cache_kv_w2048)
    mp, ms = BATCH * SEQ, DEC_BATCH * DEC_SEQ
    xp = x_prompt.reshape(mp, D_MODEL)
    xs = x_sample.reshape(ms, D_MODEL)

    g1 = norm_ffn1.reshape(DEPTH, 1, D_MODEL)
    gm = norm_mix.reshape(DEPTH, 1, D_MODEL)
    g2 = norm_ffn2.reshape(DEPTH, 1, D_MODEL)
    gf = norm_final.reshape(1, D_MODEL)
    lng = ln_v_gain.reshape(-1, 1, D_V)
    lnb = ln_v_bias.reshape(-1, 1, D_V)
    reps = CHUNK // DEC_SEQ
    ws_p = w_spatial
    bs_p = jnp.swapaxes(b_spatial, 1, 2)
    ws_s = jnp.tile(w_spatial[:, :, :DEC_SEQ, :DEC_SEQ], (1, 1, reps, reps))
    bs_s = jnp.swapaxes(jnp.tile(b_spatial[:, :, :DEC_SEQ], (1, 1, reps)), 1, 2)

    f32_weights = {"ffn1_in": w_ffn1_in, "ffn1_out": w_ffn1_out, "ffn2_in": w_ffn2_in,
                   "ffn2_out": w_ffn2_out, "qkv": w_qkv_a, "out_a": w_out_a, "uv": w_uv_b,
                   "out_b": w_out_b}
    bf16_weights = {("ffn1_in", 0): w_ffn1_in[0].astype(BF16),
                    ("ffn1_out", 0): w_ffn1_out[0].astype(BF16)}

    def jobs(*keys):
        return keys, tuple((f32_weights[name], idx) for name, idx in keys)

    def done(keys, converted):
        bf16_weights.update(zip(keys, converted, strict=True))

    def w(name, idx):
        return bf16_weights[(name, idx)]

    win_layers = [[] for _ in range(N_GROUPS_A)]
    windows, kv_s, v_rows = None, [], []
    for i in range(DEPTH):
        li = i // 2
        attention = i % 2 == 0
        last = i == DEPTH - 1

        if not attention:
            keys, casts = jobs(("uv", li), ("out_b", li), ("ffn2_in", i), ("ffn2_out", i))
        elif i == 0:
            keys, casts = jobs(("qkv", li), ("out_a", li), ("ffn2_in", i), ("ffn2_out", i))
        else:
            keys, casts = jobs(("ffn2_in", i), ("ffn2_out", i))
        xp, converted, xs, _ = _ffn(xp, i, g1, w("ffn1_in", i), w("ffn1_out", i), gf, tm=TILE,
                                    casts=casts, side=xs)
        done(keys, converted)

        keys, casts = jobs() if last else jobs(("ffn1_in", i + 1), ("ffn1_out", i + 1))
        if attention:
            qkv_g, wins, converted = _qkv_prompt(xp, i, gm, w("qkv", li), casts)
            done(keys, converted)
            for grp in range(N_GROUPS_A):
                win_layers[grp].append(wins[grp])
            parts = [_band_attention(qkv_g[grp], grp) for grp in range(N_GROUPS_A)]
            mixer = ([p[0] for p in parts], [p[1] for p in parts], w("out_a", li))
            qkv, kv = _qkv_sample(xs, i, gm, w("qkv", li))
            kv_s.append(kv)
            y = _sample_attention(qkv, caches, li)
            xs = _proj(xs, y.reshape(ms, A_WIDTH), w("out_a", li))
            xp, _, xs, _ = _ffn(xp, i, g2, w("ffn2_in", i), w("ffn2_out", i), gf, tm=TILE,
                                final=last, mixer=mixer, side=xs)
        else:
            (xp,), converted = _gmlp(xp, i, gm, w("uv", li), lng, lnb, ws_p, bs_p, w("out_b", li),
                                     tm=TILE, sample=False, casts=casts)
            done(keys, converted)
            (xs, v), _ = _gmlp(xs, i, gm, w("uv", li), lng, lnb, ws_s, bs_s, w("out_b", li),
                               tm=ms, sample=True)
            v_rows.append(v)
            keys, casts = jobs() if last else jobs(("qkv", li + 1), ("out_a", li + 1))
            stacks = tuple(tuple(layers) for layers in win_layers) if last else ()
            xp, converted, xs, stacked = _ffn(xp, i, g2, w("ffn2_in", i), w("ffn2_out", i), gf,
                                              tm=TILE, final=last, casts=casts, side=xs,
                                              stacks=stacks)
            done(keys, converted)
            if last:
                windows = stacked

    def prompt_window(grp):
        return jnp.transpose(windows[grp], (0, 1, 5, 2, 3, 4))

    def sample_rows(grp):
        return jnp.stack([kv.reshape(DEC_BATCH, DEC_SEQ, N_GROUPS_A, 2, N_SLOTS, HEAD_DIM)[:, :, grp]
                          for kv in kv_s])

    return (xp.reshape(BATCH, SEQ, D_MODEL), xs.reshape(DEC_BATCH, DEC_SEQ, D_MODEL),
            prompt_window(0), prompt_window(1), prompt_window(2),
            sample_rows(0), sample_rows(1), sample_rows(2),
            jnp.stack(v_rows).reshape(len(v_rows), DEC_BATCH, DEC_SEQ, D_V))
```

```python
import functools

import jax
import jax.numpy as jnp
from jax import lax
from jax.experimental import pallas as pl
from jax.experimental.pallas import tpu as pltpu

F32 = jnp.float32
BF16 = jnp.bfloat16

D_MODEL = 1024
BATCH = 4
SEQ = 4096
DEPTH = 4
DEC_BATCH = 32
DEC_SEQ = 4
HEAD_DIM = 64
N_SLOTS = 8
DIL_WINDOWS = (128, 512, 2048)
DIL_RATES = (1, 4, 16)
N_GROUPS_A = 3
A_WIDTH = N_SLOTS * HEAD_DIM
QKV_WIDTH = 3 * N_GROUPS_A * A_WIDTH
KV_WIDTH = 2 * N_GROUPS_A * A_WIDTH
BAND = 128
CHUNK = 128
D_V = 3072
N_GROUPS_B = 8
GROUP_B = D_V // N_GROUPS_B
D_FF = 2816
RMS_EPS = 1e-6
LN_EPS = 1e-5
NEG = -1e30
LOG2_E = 1.4426950408889634

VMEM_LIMIT_BYTES = 56 * 1024 * 1024
LANES = 128
BF16_ROWS = 16
FREE_STRIDE = 4
STACK_COLS = 256
TILE = 512
TILES_PER_SEQ = SEQ // TILE
MM_COLS = 256
FF_COLS = MM_COLS
UV_COLS = 768
Q_ROWS = 512
NEW_PAD = 16


def _params(n_axes):
    return pltpu.CompilerParams(dimension_semantics=("arbitrary",) * n_axes,
                                vmem_limit_bytes=VMEM_LIMIT_BYTES)


def _resident(shape, index_map):
    return pl.BlockSpec(shape, index_map, pipeline_mode=pl.Buffered(1))


def _rms(x, g):
    return x * lax.rsqrt(jnp.mean(x * x, axis=-1, keepdims=True) + RMS_EPS) * g


def _dot(a, b):
    return jnp.dot(a, b, preferred_element_type=F32)


def _dot_nt(a, b):
    return lax.dot_general(a, b, (((1,), (1,)), ((), ())), preferred_element_type=F32)


def _cast_plan(src, layer, n_steps):
    rows, cols = src.shape[1:]
    rb = rows // n_steps
    if rows % n_steps or rb % BF16_ROWS:
        rb = LANES
    nb = rows // rb
    assert rows % rb == 0 and nb <= n_steps
    in_spec = pl.BlockSpec((None, rb, cols), lambda i: (layer, jnp.minimum(i, nb - 1), 0))
    out_spec = pl.BlockSpec((rb, cols), lambda i: (jnp.minimum(i, nb - 1), 0))
    return in_spec, out_spec, jax.ShapeDtypeStruct((rows, cols), BF16)


def _cast_rows(src_refs, dst_refs):
    for src, dst in zip(src_refs, dst_refs, strict=True):
        dst[...] = src[...].astype(BF16)


def _stack_plan(per_layer, n_steps):
    n_batch, *mid, width = per_layer[0].shape
    wb = min(width, STACK_COLS)
    per_batch = width // wb
    nb = n_batch * per_batch
    assert width % wb == 0 and nb <= n_steps

    def at(i):
        k = jnp.minimum(i, nb - 1)
        return k // per_batch, k % per_batch

    in_spec = pl.BlockSpec((None, *mid, wb), lambda i: (at(i)[0], 0, 0, 0, at(i)[1]))
    out_spec = pl.BlockSpec((len(per_layer), None, *mid, wb),
                            lambda i: (0, at(i)[0], 0, 0, 0, at(i)[1]))
    out_shape = jax.ShapeDtypeStruct((len(per_layer), n_batch, *mid, width), F32)
    return [in_spec] * len(per_layer), out_spec, out_shape


def _stack_layers(src_refs, dst_refs):
    per = len(src_refs) // max(len(dst_refs), 1)
    for k, dst in enumerate(dst_refs):
        for layer in range(per):
            dst[layer] = src_refs[k * per + layer][...]


def _merge(outs, lses):
    mx = jnp.maximum(jnp.maximum(lses[0], lses[1]), lses[2])
    e = [jnp.exp(l - mx) for l in lses]
    den = e[0] + e[1] + e[2]
    return (e[0] / den) * outs[0] + (e[1] / den) * outs[1] + (e[2] / den) * outs[2]


def _position_order(slab_ref, slab2_ref, src_ref, d):
    n = TILE // d
    n_cols = A_WIDTH // LANES
    two_pass = d > FREE_STRIDE
    per = d // FREE_STRIDE if two_pass else d
    m = TILE // FREE_STRIDE
    dst = slab2_ref if two_pass else slab_ref
    for k in range(d):
        r0, r1 = divmod(k, per)
        for cc in range(n_cols):
            dst[cc, pl.ds(r0 * m * two_pass + r1, n, stride=per), :] = (
                src_ref[k * n:(k + 1) * n, cc * LANES:(cc + 1) * LANES])
    if two_pass:
        for r0 in range(FREE_STRIDE):
            for cc in range(n_cols):
                slab_ref[cc, pl.ds(r0, m, stride=FREE_STRIDE), :] = slab2_ref[cc, r0 * m:(r0 + 1) * m, :]
    return [slab_ref[cc] for cc in range(n_cols)]


def _merged_attention(o0, o1, o2, l0, l1, l2, slab_ref):
    cols = [slice(cc * LANES, (cc + 1) * LANES) for cc in range(A_WIDTH // LANES)]
    outs = [[o0[:, c] for c in cols],
            _position_order(slab_ref.at[0], None, o1, DIL_RATES[1]),
            _position_order(slab_ref.at[1], slab_ref.at[4], o2, DIL_RATES[2])]
    lses = [[l0[:, c] for c in cols],
            _position_order(slab_ref.at[2], None, l1, DIL_RATES[1]),
            _position_order(slab_ref.at[3], slab_ref.at[5], l2, DIL_RATES[2])]
    return jnp.concatenate(
        [_merge([o[cc] for o in outs], [l[cc] for l in lses]) for cc in range(len(cols))], axis=1)


def _ffn_kernel(*refs, final, mixer, n_cast, n_stack_src, n_stack, side):
    x_ref, g_ref, win_ref, wout_ref, gf_ref = refs[:5]
    refs = refs[5:]
    if mixer:
        *attn_refs, wmix_ref = refs[:7]
        refs = refs[7:]
    cast_src, refs = refs[:n_cast], refs[n_cast:]
    stack_src, refs = refs[:n_stack_src], refs[n_stack_src:]
    if side:
        xs_ref, refs = refs[0], refs[1:]
    o_ref, cast_dst, refs = refs[0], refs[1:n_cast + 1], refs[n_cast + 1:]
    stack_dst, refs = refs[:n_stack], refs[n_stack:]
    if side:
        os_ref, refs = refs[0], refs[1:]
    a_ref = refs[0]
    _stack_layers(stack_src, stack_dst)

    def half_step(x):
        rows = x.shape[0]
        h = _rms(x, g_ref[...]).astype(BF16)
        for c in range(D_FF // FF_COLS):
            lo = c * FF_COLS
            gate = _dot(h, win_ref[:, lo:lo + FF_COLS])
            up = _dot(h, win_ref[:, D_FF + lo:D_FF + lo + FF_COLS])
            a_ref[:rows, lo:lo + FF_COLS] = (gate * jax.nn.sigmoid(gate) * up).astype(BF16)
        y = x + 0.5 * _dot(a_ref[:rows], wout_ref[...])
        return _rms(y, gf_ref[...]) if final else y

    _cast_rows(cast_src, cast_dst)
    x = x_ref[...]
    if mixer:
        x = x + _dot(_merged_attention(*attn_refs, refs[1]).astype(BF16), wmix_ref[...])
    o_ref[...] = half_step(x)
    if side:
        @pl.when(pl.program_id(0) == pl.num_programs(0) - 1)
        def _():
            os_ref[...] = half_step(xs_ref[...])


def _ffn(x, layer, g, w_in, w_out, g_final, *, tm, final=False, mixer=None, casts=(), side=None,
         stacks=()):
    m = x.shape[0]
    steps = m // tm
    plans = [_cast_plan(src, lyr, steps) for src, lyr in casts]
    stack_plans = [_stack_plan(per_layer, steps) for per_layer in stacks]
    stack_in = [spec for p in stack_plans for spec in p[0]]
    stack_args = [a for per_layer in stacks for a in per_layer]
    in_specs = [
        pl.BlockSpec((tm, D_MODEL), lambda i: (i, 0)),
        pl.BlockSpec((None, 1, D_MODEL), lambda i: (layer, 0, 0)),
        _resident((D_MODEL, 2 * D_FF), lambda i: (0, 0)),
        _resident((D_FF, D_MODEL), lambda i: (0, 0)),
        pl.BlockSpec((1, D_MODEL), lambda i: (0, 0)),
    ]
    args = [x, g, w_in, w_out, g_final]
    scratch = [pltpu.VMEM((tm, D_FF), BF16)]
    if mixer is not None:
        outs, lses, w_mix = mixer
        in_specs += [pl.BlockSpec((tm, A_WIDTH), lambda i: (i, 0))] * 6
        in_specs.append(_resident((A_WIDTH, D_MODEL), lambda i: (0, 0)))
        args += [*outs, *lses, w_mix]
        scratch.append(pltpu.VMEM((6, A_WIDTH // LANES, TILE, LANES), F32))
    side_args, side_specs, side_shapes = [], [], []
    if side is not None:
        side_args = [side]
        side_specs = [pl.BlockSpec(side.shape, lambda i: (0, 0))]
        side_shapes = [jax.ShapeDtypeStruct(side.shape, F32)]
    res = pl.pallas_call(
        functools.partial(_ffn_kernel, final=final, mixer=mixer is not None, n_cast=len(plans),
                          n_stack_src=len(stack_in), n_stack=len(stack_plans),
                          side=side is not None),
        grid=(steps,),
        in_specs=in_specs + [p[0] for p in plans] + stack_in + side_specs,
        out_specs=([pl.BlockSpec((tm, D_MODEL), lambda i: (i, 0))] + [p[1] for p in plans]
                   + [p[1] for p in stack_plans] + side_specs),
        out_shape=([jax.ShapeDtypeStruct((m, D_MODEL), F32)] + [p[2] for p in plans]
                   + [p[2] for p in stack_plans] + side_shapes),
        scratch_shapes=scratch,
        compiler_params=_params(1),
        name="ffn",
    )(*args, *[src for src, _ in casts], *stack_args, *side_args)
    n = 1 + len(plans)
    ns = n + len(stack_plans)
    return res[0], res[1:n], (res[ns] if side is not None else None), res[n:ns]


def _qkv_sample_kernel(x_ref, g_ref, w_ref, qkv_ref, kv_ref):
    h = _rms(x_ref[...], g_ref[...]).astype(BF16)
    for c in range(QKV_WIDTH // A_WIDTH):
        lo = c * A_WIDTH
        part, grp = divmod(c, N_GROUPS_A)
        y = _dot(h, w_ref[:, lo:lo + A_WIDTH])
        if part == 0:
            qkv_ref[:, lo:lo + A_WIDTH] = (y * (HEAD_DIM ** -0.5)).astype(BF16)
        else:
            qkv_ref[:, lo:lo + A_WIDTH] = y.astype(BF16)
            dst = (2 * grp + part - 1) * A_WIDTH
            kv_ref[:, dst:dst + A_WIDTH] = y


def _qkv_sample(x, layer, g, w):
    m = x.shape[0]
    return pl.pallas_call(
        _qkv_sample_kernel,
        grid=(1,),
        in_specs=[
            pl.BlockSpec((m, D_MODEL), lambda i: (0, 0)),
            pl.BlockSpec((None, 1, D_MODEL), lambda i: (layer, 0, 0)),
            pl.BlockSpec((D_MODEL, QKV_WIDTH), lambda i: (0, 0)),
        ],
        out_specs=[
            pl.BlockSpec((m, QKV_WIDTH), lambda i: (0, 0)),
            pl.BlockSpec((m, KV_WIDTH), lambda i: (0, 0)),
        ],
        out_shape=[jax.ShapeDtypeStruct((m, QKV_WIDTH), BF16),
                   jax.ShapeDtypeStruct((m, KV_WIDTH), F32)],
        compiler_params=_params(1),
        name="qkv_sample",
    )(x, g, w)


def _regroup_rows(slab_ref, slab2_ref, y, dst_ref, col0, d):
    n = TILE // d
    cols = [slice(cc * LANES, (cc + 1) * LANES) for cc in range(A_WIDTH // LANES)]
    for cc, c in enumerate(cols):
        slab_ref[cc] = y[:, c]
    if d > FREE_STRIDE:
        m = TILE // FREE_STRIDE
        for r0 in range(FREE_STRIDE):
            for cc in range(len(cols)):
                slab2_ref[cc, r0 * m:(r0 + 1) * m, :] = slab_ref[cc, pl.ds(r0, m, stride=FREE_STRIDE), :]
        src, per, step = slab2_ref, d // FREE_STRIDE, m
    else:
        src, per, step = slab_ref, d, 0
    for k in range(d):
        r0, r1 = divmod(k, per)
        for cc, c in enumerate(cols):
            dst_ref[k * n:(k + 1) * n, col0 + c.start:col0 + c.stop] = (
                src[cc, pl.ds(r0 * step + r1, n, stride=per), :].astype(BF16))


def _qkv_prompt_kernel(*refs, n_cast):
    x_ref, g_ref, w_ref = refs[:3]
    cast_src = refs[3:3 + n_cast]
    outs = refs[3 + n_cast:]
    dst, win = outs[:N_GROUPS_A], outs[N_GROUPS_A:2 * N_GROUPS_A]
    cast_dst = outs[2 * N_GROUPS_A:2 * N_GROUPS_A + n_cast]
    slab_ref = outs[2 * N_GROUPS_A + n_cast]
    _cast_rows(cast_src, cast_dst)
    h = _rms(x_ref[...], g_ref[...]).astype(BF16)
    for c in range(QKV_WIDTH // A_WIDTH):
        part, grp = divmod(c, N_GROUPS_A)
        y = _dot(h, w_ref[:, c * A_WIDTH:(c + 1) * A_WIDTH])
        if part == 0:
            y = y * (HEAD_DIM ** -0.5 * LOG2_E)
        col0 = part * A_WIDTH
        if grp == 0:
            dst[0][:, col0:col0 + A_WIDTH] = y.astype(BF16)
        else:
            _regroup_rows(slab_ref.at[(grp - 1) * 3 + part], slab_ref.at[6 + part], y, dst[grp],
                          col0, DIL_RATES[grp])
        if part > 0:
            keep = min(DIL_WINDOWS[grp], TILE)
            win[grp][part - 1] = y[TILE - keep:].T.reshape(N_SLOTS, HEAD_DIM, keep)


def _qkv_prompt(x, layer, g, w, casts=()):
    m = x.shape[0]
    steps = m // TILE
    plans = [_cast_plan(src, lyr, steps) for src, lyr in casts]
    row_spec = pl.BlockSpec((TILE, 3 * A_WIDTH), lambda i: (i, 0))
    win_specs, win_shapes = [], []
    for grp in range(N_GROUPS_A):
        keep = min(DIL_WINDOWS[grp], TILE)
        first_tile = TILES_PER_SEQ - max(DIL_WINDOWS[grp] // TILE, 1)
        win_specs.append(pl.BlockSpec(
            (None, 2, N_SLOTS, HEAD_DIM, keep),
            lambda i, ft=first_tile: (i // TILES_PER_SEQ, 0, 0, 0,
                                      jnp.maximum(i % TILES_PER_SEQ - ft, 0))))
        win_shapes.append(jax.ShapeDtypeStruct(
            (BATCH, 2, N_SLOTS, HEAD_DIM, DIL_WINDOWS[grp]), F32))
    res = pl.pallas_call(
        functools.partial(_qkv_prompt_kernel, n_cast=len(plans)),
        grid=(steps,),
        in_specs=[
            pl.BlockSpec((TILE, D_MODEL), lambda i: (i, 0)),
            pl.BlockSpec((None, 1, D_MODEL), lambda i: (layer, 0, 0)),
            _resident((D_MODEL, QKV_WIDTH), lambda i: (0, 0)),
        ] + [p[0] for p in plans],
        out_specs=[row_spec] * N_GROUPS_A + win_specs + [p[1] for p in plans],
        out_shape=([jax.ShapeDtypeStruct((m, 3 * A_WIDTH), BF16)] * N_GROUPS_A + win_shapes
                   + [p[2] for p in plans]),
        scratch_shapes=[pltpu.VMEM((9, A_WIDTH // LANES, TILE, LANES), F32)],
        compiler_params=_params(1),
        name="qkv_prompt",
    )(x, g, w, *[src for src, _ in casts])
    return res[:N_GROUPS_A], res[N_GROUPS_A:2 * N_GROUPS_A], res[2 * N_GROUPS_A:]


def _band_kernel(mask_ref, q_ref, kp_ref, ko_ref, vp_ref, vo_ref, o_ref, l_ref, s_ref, p_ref, m_ref):
    n = q_ref.shape[1]
    q_rows = q_ref.shape[0] * n
    n_sub, n_pair = q_rows // BAND, A_WIDTH // LANES
    bias = mask_ref[0]
    bias_first = mask_ref[(pl.program_id(2) == 0).astype(jnp.int32)]
    low = lax.broadcasted_iota(jnp.int32, (1, LANES), 1) < HEAD_DIM
    q_all = q_ref[...].reshape(q_rows, A_WIDTH)
    ko_all = ko_ref[...].reshape(q_rows, A_WIDTH)
    vo_all = vo_ref[...].reshape(q_rows, A_WIDTH)
    kp_all = kp_ref[...].reshape(BAND, A_WIDTH)
    vp_all = vp_ref[...].reshape(BAND, A_WIDTH)

    def keys(prev, own, j, cs):
        if j == 0:
            return jnp.concatenate([prev[:, cs], own[:BAND, cs]], axis=0)
        return own[(j - 1) * BAND:(j + 1) * BAND, cs]

    for pr in range(n_pair):
        cs = slice(pr * LANES, (pr + 1) * LANES)
        for j in range(n_sub):
            q = q_all[j * BAND:(j + 1) * BAND, cs]
            zero = jnp.zeros_like(q)
            q_ab = jnp.concatenate([jnp.where(low, q, zero), jnp.where(low, zero, q)], axis=0)
            s = _dot_nt(q_ab, keys(kp_all, ko_all, j, cs))
            s_ref[pr * n_sub + j] = s + (bias_first if j == 0 else bias)

    for u in range(n_pair * n_sub):
        mx = jnp.max(s_ref[u], axis=-1, keepdims=True)
        p_ref[u] = jnp.exp2(s_ref[u] - mx).astype(BF16)
        m_ref[u] = jnp.where(low, mx[:BAND], mx[BAND:])

    one = jnp.ones((2 * BAND, LANES), BF16)
    for pr in range(n_pair):
        cs = slice(pr * LANES, (pr + 1) * LANES)
        for j in range(n_sub):
            u = pr * n_sub + j
            vv = keys(vp_all, vo_all, j, cs)
            oa = _dot(p_ref[u, :BAND], jnp.where(low, vv, one))
            ob = _dot(p_ref[u, BAND:], jnp.where(low, one, vv))
            den = pltpu.roll(jnp.where(low, ob, oa), HEAD_DIM, axis=1)
            lse = (m_ref[u] + jnp.log2(den)) * (1.0 / LOG2_E)
            tiles = slice(j * (BAND // n), (j + 1) * (BAND // n))
            o_ref[tiles, :, cs] = (jnp.where(low, oa, ob) / den).reshape(BAND // n, n, LANES)
            l_ref[tiles, :, cs] = lse.reshape(BAND // n, n, LANES)


def _band_attention(qkv, grp):
    d = DIL_RATES[grp]
    n = min(TILE // d, BAND)
    pieces = BATCH * SEQ // (d * n)
    q_rows = min(Q_ROWS, SEQ // d)
    nb = SEQ // (d * q_rows)
    units = (q_rows // BAND) * (A_WIDTH // LANES)
    view = qkv.reshape(pieces, d, n, 3 * A_WIDTH)

    def own(part):
        return pl.BlockSpec((q_rows // n, None, n, A_WIDTH), lambda b, r, i: (b * nb + i, r, 0, part))

    def prev(part):
        return pl.BlockSpec((BAND // n, None, n, A_WIDTH),
                            lambda b, r, i: (jnp.maximum((b * nb + i) * (q_rows // BAND) - 1, 0),
                                             r, 0, part))

    qi = lax.broadcasted_iota(jnp.int32, (2 * BAND, 2 * BAND), 0) % BAND
    ki = lax.broadcasted_iota(jnp.int32, (2 * BAND, 2 * BAND), 1)
    band = (ki >= qi) & (ki <= qi + BAND)
    masks = jnp.where(jnp.stack([band, band & (ki >= BAND)]), 0.0, NEG).astype(F32)

    out_spec = pl.BlockSpec((q_rows // n, None, n, A_WIDTH), lambda b, r, i: (b * nb + i, r, 0, 0))
    out_shape = jax.ShapeDtypeStruct((pieces, d, n, A_WIDTH), F32)
    o, lse = pl.pallas_call(
        _band_kernel,
        grid=(BATCH, d, nb),
        in_specs=[pl.BlockSpec(masks.shape, lambda b, r, i: (0, 0, 0)),
                  own(0), prev(1), own(1), prev(2), own(2)],
        out_specs=[out_spec, out_spec],
        out_shape=[out_shape, out_shape],
        scratch_shapes=[pltpu.VMEM((units, 2 * BAND, 2 * BAND), F32),
                        pltpu.VMEM((units, 2 * BAND, 2 * BAND), BF16),
                        pltpu.VMEM((units, BAND, LANES), F32)],
        compiler_params=_params(3),
        name=f"band_attention_d{d}",
    )(masks, view, view, view, view, view)
    return o.reshape(BATCH * SEQ, A_WIDTH), lse.reshape(BATCH * SEQ, A_WIDTH)


def _sample_attn_kernel(qkv_ref, c0_ref, c1_ref, c2_ref, y_ref):
    n_rows = DEC_SEQ * N_SLOTS
    qkv = qkv_ref[...].astype(F32)
    row_h = lax.broadcasted_iota(jnp.int32, (n_rows, A_WIDTH), 0) % N_SLOTS
    col_h = lax.broadcasted_iota(jnp.int32, (n_rows, A_WIDTH), 1) // HEAD_DIM
    own_head = row_h == col_h
    pad = jnp.zeros((NEW_PAD - DEC_SEQ, A_WIDTH), F32)

    def reach(n_keys, offset, d):
        t = lax.broadcasted_iota(jnp.int32, (n_rows, n_keys), 0) // N_SLOTS
        back = t - lax.broadcasted_iota(jnp.int32, (n_rows, n_keys), 1) - offset
        return (back >= 0) & (back <= BAND * d) & ((back & (d - 1)) == 0)

    outs, lses = [], []
    for grp, c_ref in enumerate((c0_ref, c1_ref, c2_ref)):
        d, window = DIL_RATES[grp], DIL_WINDOWS[grp]
        q = qkv[:, grp * A_WIDTH:(grp + 1) * A_WIDTH]
        k_new = qkv[:, (N_GROUPS_A + grp) * A_WIDTH:(N_GROUPS_A + grp + 1) * A_WIDTH]
        v_new = qkv[:, (2 * N_GROUPS_A + grp) * A_WIDTH:(2 * N_GROUPS_A + grp + 1) * A_WIDTH]
        q_rep = jnp.concatenate(
            [jnp.broadcast_to(q[t:t + 1], (N_SLOTS, A_WIDTH)) for t in range(DEC_SEQ)], axis=0)
        q_bd = jnp.where(own_head, q_rep, 0.0).astype(BF16)
        k_new = jnp.concatenate([k_new, pad], axis=0).astype(BF16)
        v_new = jnp.concatenate([v_new, pad], axis=0).astype(BF16)
        s_c = jnp.where(reach(window, -window, d), _dot(q_bd, c_ref[0].astype(BF16)), NEG)
        s_n = jnp.where(reach(NEW_PAD, 0, d), _dot_nt(q_bd, k_new), NEG)
        mx = jnp.maximum(jnp.max(s_c, axis=-1, keepdims=True), jnp.max(s_n, axis=-1, keepdims=True))
        p_c = jnp.exp(s_c - mx)
        p_n = jnp.exp(s_n - mx)
        den = jnp.sum(p_c, axis=-1, keepdims=True) + jnp.sum(p_n, axis=-1, keepdims=True)
        o = (_dot_nt(p_c.astype(BF16), c_ref[1].astype(BF16)) + _dot(p_n.astype(BF16), v_new)) / den
        lse = jnp.broadcast_to(mx + jnp.log(den), (n_rows, A_WIDTH))
        o = jnp.where(own_head, o, 0.0)
        lse = jnp.where(own_head, lse, 0.0)
        outs.append(jnp.concatenate(
            [jnp.sum(o[t * N_SLOTS:(t + 1) * N_SLOTS], axis=0, keepdims=True) for t in range(DEC_SEQ)],
            axis=0))
        lses.append(jnp.concatenate(
            [jnp.sum(lse[t * N_SLOTS:(t + 1) * N_SLOTS], axis=0, keepdims=True) for t in range(DEC_SEQ)],
            axis=0))
    y_ref[...] = _merge(outs, lses)


def _sample_attention(qkv, caches, li):
    views = [jnp.transpose(c, (0, 1, 3, 4, 5, 2)).reshape(-1, DEC_BATCH, 2, A_WIDTH, c.shape[2])
             for c in caches]
    return pl.pallas_call(
        _sample_attn_kernel,
        grid=(DEC_BATCH,),
        in_specs=[pl.BlockSpec((None, DEC_SEQ, QKV_WIDTH), lambda b: (b, 0, 0))] + [
            pl.BlockSpec((None, None, 2, A_WIDTH, w), lambda b: (li, b, 0, 0, 0)) for w in DIL_WINDOWS],
        out_specs=pl.BlockSpec((None, DEC_SEQ, A_WIDTH), lambda b: (b, 0, 0)),
        out_shape=jax.ShapeDtypeStruct((DEC_BATCH, DEC_SEQ, A_WIDTH), F32),
        compiler_params=_params(1),
        name="sample_attention",
    )(qkv.reshape(DEC_BATCH, DEC_SEQ, QKV_WIDTH), *views)


def _proj_kernel(x_ref, y_ref, w_ref, out_ref):
    out_ref[...] = x_ref[...] + _dot(y_ref[...].astype(BF16), w_ref[...])


def _proj(x, y, w):
    m = x.shape[0]
    return pl.pallas_call(
        _proj_kernel,
        grid=(1,),
        in_specs=[pl.BlockSpec((m, D_MODEL), lambda i: (0, 0)),
                  pl.BlockSpec((m, A_WIDTH), lambda i: (0, 0)),
                  pl.BlockSpec((A_WIDTH, D_MODEL), lambda i: (0, 0))],
        out_specs=pl.BlockSpec((m, D_MODEL), lambda i: (0, 0)),
        out_shape=jax.ShapeDtypeStruct((m, D_MODEL), F32),
        compiler_params=_params(1),
        name="proj",
    )(x, y, w)


def _gelu(x):
    a0 = -2.0 * (2.0 / jnp.pi) ** 0.5 * LOG2_E
    return x / (1.0 + jnp.exp2(x * (a0 + (a0 * 0.044715) * (x * x))))


def _gmlp_kernel(*refs, tm, sample, n_cast):
    x_ref, g_ref, wuv_ref, lng_ref, lnb_ref, ws_ref, bs_ref, wo_ref = refs[:8]
    cast_src = refs[8:8 + n_cast]
    refs = refs[8 + n_cast:]
    out_ref = refs[0]
    if sample:
        v_ref = refs[1]
    refs = refs[2:] if sample else refs[1:]
    cast_dst = refs[:n_cast]
    zv_ref, vn_ref, um_ref = refs[n_cast:]
    _cast_rows(cast_src, cast_dst)
    x = x_ref[...]
    h = _rms(x, g_ref[...]).astype(BF16)
    n_uv = D_V // UV_COLS

    tot = jnp.zeros((tm, 1), F32)
    for c in range(n_uv):
        lo = c * UV_COLS
        z = _gelu(_dot(h, wuv_ref[:, D_V + lo:D_V + lo + UV_COLS]))
        zv_ref[:, lo:lo + UV_COLS] = z
        tot = tot + jnp.sum(z, axis=-1, keepdims=True)
    mu = tot / D_V
    sq = jnp.zeros((tm, 1), F32)
    for c in range(n_uv):
        zc = zv_ref[:, c * UV_COLS:(c + 1) * UV_COLS] - mu
        sq = sq + jnp.sum(zc * zc, axis=-1, keepdims=True)
    rstd = lax.rsqrt(sq / D_V + LN_EPS)
    for c in range(n_uv):
        cols = slice(c * UV_COLS, (c + 1) * UV_COLS)
        vn = (zv_ref[:, cols] - mu) * rstd * lng_ref[:, cols] + lnb_ref[:, cols]
        vn_ref[:, cols] = vn.astype(BF16)
        if sample:
            v_ref[:, cols] = vn

    ri = lax.broadcasted_iota(jnp.int32, (CHUNK, CHUNK), 0)
    ci = lax.broadcasted_iota(jnp.int32, (CHUNK, CHUNK), 1)
    causal = ci <= ri
    if sample:
        causal = causal & ((ri // DEC_SEQ) == (ci // DEC_SEQ))
    groups_per_mm = UV_COLS // GROUP_B
    for c in range(n_uv):
        u = _gelu(_dot(h, wuv_ref[:, c * UV_COLS:(c + 1) * UV_COLS]))
        for gl in range(groups_per_mm):
            grp = c * groups_per_mm + gl
            w = jnp.where(causal, ws_ref[grp], 0.0).astype(BF16)
            bias = bs_ref[:, grp:grp + 1]
            cols = slice(grp * GROUP_B, (grp + 1) * GROUP_B)
            for n in range(tm // CHUNK):
                rows = slice(n * CHUNK, (n + 1) * CHUNK)
                mixed = _dot(w, vn_ref[rows, cols]) + bias
                um_ref[rows, cols] = (u[rows, gl * GROUP_B:(gl + 1) * GROUP_B] * mixed).astype(BF16)
    out_ref[...] = x + _dot(um_ref[...], wo_ref[...])


def _gmlp(x, layer, g, w_uv, ln_g, ln_b, w_s, b_s, w_out, *, tm, sample, casts=()):
    m = x.shape[0]
    li = layer // 2
    steps = m // tm
    plans = [_cast_plan(src, lyr, steps) for src, lyr in casts]
    out_specs = [pl.BlockSpec((tm, D_MODEL), lambda i: (i, 0))]
    out_shape = [jax.ShapeDtypeStruct((m, D_MODEL), F32)]
    if sample:
        out_specs.append(pl.BlockSpec((tm, D_V), lambda i: (i, 0)))
        out_shape.append(jax.ShapeDtypeStruct((m, D_V), F32))
    n_main = len(out_specs)
    res = pl.pallas_call(
        functools.partial(_gmlp_kernel, tm=tm, sample=sample, n_cast=len(plans)),
        grid=(steps,),
        in_specs=[
            pl.BlockSpec((tm, D_MODEL), lambda i: (i, 0)),
            pl.BlockSpec((None, 1, D_MODEL), lambda i: (layer, 0, 0)),
            _resident((D_MODEL, 2 * D_V), lambda i: (0, 0)),
            pl.BlockSpec((None, 1, D_V), lambda i: (li, 0, 0)),
            pl.BlockSpec((None, 1, D_V), lambda i: (li, 0, 0)),
            pl.BlockSpec((None, N_GROUPS_B, CHUNK, CHUNK), lambda i: (li, 0, 0, 0)),
            pl.BlockSpec((None, CHUNK, N_GROUPS_B), lambda i: (li, 0, 0)),
            _resident((D_V, D_MODEL), lambda i: (0, 0)),
        ] + [p[0] for p in plans],
        out_specs=out_specs + [p[1] for p in plans],
        out_shape=out_shape + [p[2] for p in plans],
        scratch_shapes=[pltpu.VMEM((tm, D_V), F32), pltpu.VMEM((tm, D_V), BF16),
                        pltpu.VMEM((tm, D_V), BF16)],
        compiler_params=_params(1),
        name="gmlp",
    )(x, g, w_uv, ln_g, ln_b, w_s, b_s, w_out, *[src for src, _ in casts])
    return res[:n_main], res[n_main:]


def kernel(x_prompt, x_sample, cache_kv_w128, cache_kv_w512, cache_kv_w2048, norm_ffn1, w_ffn1_in,
           w_ffn1_out, norm_mix, norm_ffn2, w_ffn2_in, w_ffn2_out, w_qkv_a, w_out_a, w_uv_b,
           ln_v_gain, ln_v_bias, w_spatial, b_spatial, w_out_b, norm_final):
    caches = (cache_kv_w128, cache_kv_w512, cache_kv_w2048)
    mp, ms = BATCH * SEQ, DEC_BATCH * DEC_SEQ
    xp = x_prompt.reshape(mp, D_MODEL)
    xs = x_sample.reshape(ms, D_MODEL)

    g1 = norm_ffn1.reshape(DEPTH, 1, D_MODEL)
    gm = norm_mix.reshape(DEPTH, 1, D_MODEL)
    g2 = norm_ffn2.reshape(DEPTH, 1, D_MODEL)
    gf = norm_final.reshape(1, D_MODEL)
    lng = ln_v_gain.reshape(-1, 1, D_V)
    lnb = ln_v_bias.reshape(-1, 1, D_V)
    reps = CHUNK // DEC_SEQ
    ws_p = w_spatial
    bs_p = jnp.swapaxes(b_spatial, 1, 2)
    ws_s = jnp.tile(w_spatial[:, :, :DEC_SEQ, :DEC_SEQ], (1, 1, reps, reps))
    bs_s = jnp.swapaxes(jnp.tile(b_spatial[:, :, :DEC_SEQ], (1, 1, reps)), 1, 2)

    f32_weights = {"ffn1_in": w_ffn1_in, "ffn1_out": w_ffn1_out, "ffn2_in": w_ffn2_in,
                   "ffn2_out": w_ffn2_out, "qkv": w_qkv_a, "out_a": w_out_a, "uv": w_uv_b,
                   "out_b": w_out_b}
    bf16_weights = {("ffn1_in", 0): w_ffn1_in[0].astype(BF16),
                    ("ffn1_out", 0): w_ffn1_out[0].astype(BF16)}

    def jobs(*keys):
        return keys, tuple((f32_weights[name], idx) for name, idx in keys)

    def done(keys, converted):
        bf16_weights.update(zip(keys, converted, strict=True))

    def w(name, idx):
        return bf16_weights[(name, idx)]

    win_layers = [[] for _ in range(N_GROUPS_A)]
    windows, kv_s, v_rows = None, [], []
    for i in range(DEPTH):
        li = i // 2
        attention = i % 2 == 0
        last = i == DEPTH - 1

        if not attention:
            keys, casts = jobs(("uv", li), ("out_b", li), ("ffn2_in", i), ("ffn2_out", i))
        elif i == 0:
            keys, casts = jobs(("qkv", li), ("out_a", li), ("ffn2_in", i), ("ffn2_out", i))
        else:
            keys, casts = jobs(("ffn2_in", i), ("ffn2_out", i))
        xp, converted, xs, _ = _ffn(xp, i, g1, w("ffn1_in", i), w("ffn1_out", i), gf, tm=TILE,
                                    casts=casts, side=xs)
        done(keys, converted)

        keys, casts = jobs() if last else jobs(("ffn1_in", i + 1), ("ffn1_out", i + 1))
        if attention:
            qkv_g, wins, converted = _qkv_prompt(xp, i, gm, w("qkv", li), casts)
            done(keys, converted)
            for grp in range(N_GROUPS_A):
                win_layers[grp].append(wins[grp])
            parts = [_band_attention(qkv_g[grp], grp) for grp in range(N_GROUPS_A)]
            mixer = ([p[0] for p in parts], [p[1] for p in parts], w("out_a", li))
            qkv, kv = _qkv_sample(xs, i, gm, w("qkv", li))
            kv_s.append(kv)
            y = _sample_attention(qkv, caches, li)
            xs = _proj(xs, y.reshape(ms, A_WIDTH), w("out_a", li))
            xp, _, xs, _ = _ffn(xp, i, g2, w("ffn2_in", i), w("ffn2_out", i), gf, tm=TILE,
                                final=last, mixer=mixer, side=xs)
        else:
            (xp,), converted = _gmlp(xp, i, gm, w("uv", li), lng, lnb, ws_p, bs_p, w("out_b", li),
                                     tm=TILE, sample=False, casts=casts)
            done(keys, converted)
            (xs, v), _ = _gmlp(xs, i, gm, w("uv", li), lng, lnb, ws_s, bs_s, w("out_b", li),
                               tm=ms, sample=True)
            v_rows.append(v)
            keys, casts = jobs() if last else jobs(("qkv", li + 1), ("out_a", li + 1))
            stacks = tuple(tuple(layers) for layers in win_layers) if last else ()
            xp, converted, xs, stacked = _ffn(xp, i, g2, w("ffn2_in", i), w("ffn2_out", i), gf,
                                              tm=TILE, final=last, casts=casts, side=xs,
                                              stacks=stacks)
            done(keys, converted)
            if last:
                windows = stacked

    def prompt_window(grp):
        return jnp.transpose(windows[grp], (0, 1, 5, 2, 3, 4))

    def sample_rows(grp):
        return jnp.stack([kv.reshape(DEC_BATCH, DEC_SEQ, N_GROUPS_A, 2, N_SLOTS, HEAD_DIM)[:, :, grp]
                          for kv in kv_s])

    return (xp.reshape(BATCH, SEQ, D_MODEL), xs.reshape(DEC_BATCH, DEC_SEQ, D_MODEL),
            prompt_window(0), prompt_window(1), prompt_window(2),
            sample_rows(0), sample_rows(1), sample_rows(2),
            jnp.stack(v_rows).reshape(len(v_rows), DEC_BATCH, DEC_SEQ, D_V))
```
